```python
import math
import jax, jax.numpy as jnp
from jax import lax
import numpy as np


D_MODEL = 1024
BATCH = 8
SEQ = 8192
DEPTH = 2

HEAD_DIM = 64
N_MIX_HEADS = D_MODEL // HEAD_DIM
NSA_HEADS = (3 * N_MIX_HEADS) // 8
NSA_KV_HEADS = NSA_HEADS // 3
NSA_GROUP = NSA_HEADS // NSA_KV_HEADS
FOX_HEADS = (3 * N_MIX_HEADS) // 8
CONV_GROUPS = N_MIX_HEADS - NSA_HEADS - FOX_HEADS
NSA_WIDTH = NSA_HEADS * HEAD_DIM
CONV_WIDTH = CONV_GROUPS * HEAD_DIM
FOX_WIDTH = FOX_HEADS * HEAD_DIM
MIX_WIDTH = NSA_WIDTH + CONV_WIDTH + FOX_WIDTH
CMP_BLOCK = 32
CMP_HIDDEN = 256
SEL_BLOCK = 64
SEL_TOPK = 16
WINDOW = 512
CONV_KERNEL = 31
D_FF = 256 * math.ceil(8 * D_MODEL / 3 / 256)
FFN_CONV = 3
Q_BLOCK = 128
ALPHA = (2.0 * DEPTH) ** 0.25
BETA = (8.0 * DEPTH) ** -0.25
LN_EPS = 1e-5
NEG_INF = -1e30
PROJ_SIZES = (NSA_WIDTH, 6 * NSA_KV_HEADS * HEAD_DIM, 3 * NSA_HEADS, 2 * CONV_WIDTH, 3 * FOX_WIDTH, FOX_HEADS)
PROJ_WIDTH = sum(PROJ_SIZES)
PROJ_SPLITS = tuple(sum(PROJ_SIZES[:i + 1]) for i in range(len(PROJ_SIZES) - 1))

kernel_name = 'hybrid_nsa_conformer_fox_deepnorm'


def layer_norm(x, g, b):
    xf = x.astype(jnp.float32)
    mu = jnp.mean(xf, axis=-1, keepdims=True)
    var = jnp.mean(jnp.square(xf - mu), axis=-1, keepdims=True)
    return ((xf - mu) * lax.rsqrt(var + LN_EPS) * g + b).astype(x.dtype)


def causal_dwconv(x, w, b):
    k = w.shape[0]
    y = lax.conv_general_dilated(x, w[:, None, :].astype(x.dtype), window_strides=(1,), padding=[(k - 1, 0)],
                                 dimension_numbers=('NWC', 'WIO', 'NWC'), feature_group_count=x.shape[-1])
    return y + b.astype(x.dtype)


def masked_softmax(s, mask):
    p = jax.nn.softmax(jnp.where(mask, s, NEG_INF), axis=-1)
    return jnp.where(mask, p, 0.0)


def alibi_slopes(n):
    return jnp.asarray(2.0 ** (-8.0 * np.arange(1, n + 1) / n), dtype=jnp.float32)


def nsa_mixer(q, kv, gate_logits, w_cmp1, w_cmp2, pe_cmp):
    bsz, s_len, _ = q.shape
    hkv, grp, hd = NSA_KV_HEADS, NSA_GROUP, HEAD_DIM
    q = q.reshape(bsz, s_len, hkv, grp, hd).transpose(0, 2, 3, 1, 4)
    kv = kv.reshape(bsz, s_len, 6, hkv, hd).transpose(2, 0, 3, 1, 4)
    k_cmp_raw, v_cmp_raw, k_slc, v_slc, k_win, v_win = kv
    gates = jax.nn.sigmoid(gate_logits.astype(jnp.float32))
    gates = gates.reshape(bsz, s_len, hkv, grp, 3).transpose(0, 2, 3, 1, 4)

    n_cmp = s_len // CMP_BLOCK
    n_sel = s_len // SEL_BLOCK
    top_k = min(SEL_TOPK, n_sel)
    ratio = SEL_BLOCK // CMP_BLOCK

    def compress(t, i):
        blk = t.reshape(bsz, hkv, n_cmp, CMP_BLOCK, hd) + pe_cmp[i].astype(t.dtype)
        hid = jax.nn.gelu(blk.reshape(bsz, hkv, n_cmp, CMP_BLOCK * hd) @ w_cmp1[i])
        return hid @ w_cmp2[i]

    k_cmp = compress(k_cmp_raw, 0)
    v_cmp = compress(v_cmp_raw, 1)
    cmp_end = jnp.arange(n_cmp) * CMP_BLOCK + CMP_BLOCK - 1
    k_slc_blk = k_slc.reshape(bsz, hkv, n_sel, SEL_BLOCK, hd)
    v_slc_blk = v_slc.reshape(bsz, hkv, n_sel, SEL_BLOCK, hd)
    k_win_pad = jnp.pad(k_win, ((0, 0), (0, 0), (WINDOW, 0), (0, 0)))
    v_win_pad = jnp.pad(v_win, ((0, 0), (0, 0), (WINDOW, 0), (0, 0)))
    slopes = alibi_slopes(NSA_HEADS).reshape(hkv, grp)[:, :, None, None]
    scale = hd ** -0.5
    sel_ids = jnp.arange(n_sel)

    def block(i):
        q0 = i * Q_BLOCK
        qb = lax.dynamic_slice_in_dim(q, q0, Q_BLOCK, axis=3)
        gb = lax.dynamic_slice_in_dim(gates, q0, Q_BLOCK, axis=3)
        t = q0 + jnp.arange(Q_BLOCK)
        dist_c = t[:, None] - cmp_end[None, :]
        s_c = jnp.einsum('bhgqd,bhnd->bhgqn', qb, k_cmp).astype(jnp.float32) * scale
        p_c = masked_softmax(s_c - slopes * dist_c.astype(jnp.float32), dist_c >= 0)
        o_c = jnp.einsum('bhgqn,bhnd->bhgqd', p_c.astype(v_cmp.dtype), v_cmp)
        imp = p_c.sum(axis=2).reshape(bsz, hkv, Q_BLOCK, n_sel, ratio).sum(-1)
        cur = (t // SEL_BLOCK)[:, None]
        forced = (sel_ids[None, :] == 0) | (sel_ids[None, :] == cur)
        score = jnp.where(sel_ids[None, :] > cur, -1.0, jnp.where(forced, NSA_GROUP + 1.0, imp))
        _, idx = lax.top_k(score, top_k)
        gidx = idx.reshape(bsz, hkv, Q_BLOCK * top_k, 1, 1)
        k_sel = jnp.take_along_axis(k_slc_blk, gidx, axis=2).reshape(bsz, hkv, Q_BLOCK, top_k * SEL_BLOCK, hd)
        v_sel = jnp.take_along_axis(v_slc_blk, gidx, axis=2).reshape(bsz, hkv, Q_BLOCK, top_k * SEL_BLOCK, hd)
        pos_s = (idx[..., None] * SEL_BLOCK + jnp.arange(SEL_BLOCK)).reshape(bsz, hkv, Q_BLOCK, top_k * SEL_BLOCK)
        dist_s = (t[:, None] - pos_s)[:, :, None]
        s_s = jnp.einsum('bhgqd,bhqkd->bhgqk', qb, k_sel).astype(jnp.float32) * scale
        p_s = masked_softmax(s_s - slopes * dist_s.astype(jnp.float32), dist_s >= 0)
        o_s = jnp.einsum('bhgqk,bhqkd->bhgqd', p_s.astype(v_sel.dtype), v_sel)
        k_w = lax.dynamic_slice_in_dim(k_win_pad, q0, WINDOW + Q_BLOCK, axis=2)
        v_w = lax.dynamic_slice_in_dim(v_win_pad, q0, WINDOW + Q_BLOCK, axis=2)
        pos_w = q0 - WINDOW + jnp.arange(WINDOW + Q_BLOCK)
        dist_w = t[:, None] - pos_w[None, :]
        mask_w = (dist_w >= 0) & (dist_w < WINDOW) & (pos_w[None, :] >= 0)
        s_w = jnp.einsum('bhgqd,bhwd->bhgqw', qb, k_w).astype(jnp.float32) * scale
        p_w = masked_softmax(s_w - slopes * dist_w.astype(jnp.float32), mask_w)
        o_w = jnp.einsum('bhgqw,bhwd->bhgqd', p_w.astype(v_w.dtype), v_w)
        out = gb[..., 0:1] * o_c + gb[..., 1:2] * o_s + gb[..., 2:3] * o_w
        return out.astype(q.dtype)

    o = lax.map(block, jnp.arange(s_len // Q_BLOCK))
    return o.transpose(1, 0, 4, 2, 3, 5).reshape(bsz, s_len, NSA_WIDTH)


def conv_mixer(u, w_dw, b_dw, ln_g, ln_b):
    a, g = jnp.split(u, 2, axis=-1)
    h = causal_dwconv(a * jax.nn.sigmoid(g), w_dw, b_dw)
    return jax.nn.silu(layer_norm(h, ln_g, ln_b))


def fox_mixer(qkv, f_logit, b_f):
    bsz, s_len, _ = qkv.shape
    q, k, v = qkv.reshape(bsz, s_len, 3, FOX_HEADS, HEAD_DIM).transpose(2, 0, 3, 1, 4)
    log_f = jax.nn.log_sigmoid(f_logit.astype(jnp.float32) + b_f.astype(jnp.float32))
    c = jnp.cumsum(log_f, axis=1).transpose(0, 2, 1)
    pos = jnp.arange(s_len)
    scale = HEAD_DIM ** -0.5

    def block(i):
        q0 = i * Q_BLOCK
        qb = lax.dynamic_slice_in_dim(q, q0, Q_BLOCK, axis=2)
        cb = lax.dynamic_slice_in_dim(c, q0, Q_BLOCK, axis=2)
        t = q0 + jnp.arange(Q_BLOCK)
        s = jnp.einsum('bhqd,bhkd->bhqk', qb, k).astype(jnp.float32) * scale
        s = s + cb[..., None] - c[:, :, None, :]
        p = masked_softmax(s, t[:, None] >= pos[None, :])
        return jnp.einsum('bhqk,bhkd->bhqd', p.astype(v.dtype), v)

    o = lax.map(block, jnp.arange(s_len // Q_BLOCK))
    return o.transpose(1, 0, 3, 2, 4).reshape(bsz, s_len, FOX_WIDTH)


def setup_inputs(seed: int = 0) -> dict:
    key = jax.random.key(seed)
    ks = jax.random.split(key, 24)
    f32 = jnp.float32

    def nrm(k, shape, s):
        return jax.random.normal(k, shape, f32) * s

    return {
        'x': nrm(ks[0], (BATCH, SEQ, D_MODEL), 1.0),
        'ln_emb_g': 1.0 + nrm(ks[1], (D_MODEL,), 0.01),
        'ln_emb_b': nrm(ks[2], (D_MODEL,), 0.01),
        'w_in': nrm(ks[3], (DEPTH, D_MODEL, PROJ_WIDTH), D_MODEL ** -0.5),
        'b_f': jnp.linspace(1.0, 5.0, FOX_HEADS, dtype=f32) + nrm(ks[4], (DEPTH, FOX_HEADS), 0.1),
        'w_cmp1': nrm(ks[5], (DEPTH, 2, CMP_BLOCK * HEAD_DIM, CMP_HIDDEN), (CMP_BLOCK * HEAD_DIM) ** -0.5),
        'w_cmp2': nrm(ks[6], (DEPTH, 2, CMP_HIDDEN, HEAD_DIM), CMP_HIDDEN ** -0.5),
        'pe_cmp': nrm(ks[7], (DEPTH, 2, CMP_BLOCK, HEAD_DIM), 0.1),
        'w_dw': nrm(ks[8], (DEPTH, CONV_KERNEL, CONV_WIDTH), CONV_KERNEL ** -0.5),
        'b_dw': nrm(ks[9], (DEPTH, CONV_WIDTH), 0.01),
        'ln_conv_g': 1.0 + nrm(ks[10], (DEPTH, CONV_WIDTH), 0.01),
        'ln_conv_b': nrm(ks[11], (DEPTH, CONV_WIDTH), 0.01),
        'w_out': nrm(ks[12], (DEPTH, MIX_WIDTH, D_MODEL), MIX_WIDTH ** -0.5 * BETA),
        'ln1_g': 1.0 + nrm(ks[13], (DEPTH, D_MODEL), 0.01),
        'ln1_b': nrm(ks[14], (DEPTH, D_MODEL), 0.01),
        'w_ffn_in': nrm(ks[15], (DEPTH, D_MODEL, 2 * D_FF), D_MODEL ** -0.5),
        'w_ffn_conv': nrm(ks[16], (DEPTH, FFN_CONV, D_FF), FFN_CONV ** -0.5),
        'b_ffn_conv': nrm(ks[17], (DEPTH, D_FF), 0.01),
        'w_ffn_down': nrm(ks[18], (DEPTH, D_FF, D_MODEL), D_FF ** -0.5 * BETA),
        'ln2_g': 1.0 + nrm(ks[19], (DEPTH, D_MODEL), 0.01),
        'ln2_b': nrm(ks[20], (DEPTH, D_MODEL), 0.01),
    }


def reference(x, ln_emb_g, ln_emb_b, w_in, b_f, w_cmp1, w_cmp2, pe_cmp, w_dw, b_dw, ln_conv_g, ln_conv_b,
              w_out, ln1_g, ln1_b, w_ffn_in, w_ffn_conv, b_ffn_conv, w_ffn_down, ln2_g, ln2_b):
    x = layer_norm(x, ln_emb_g, ln_emb_b)
    for l in range(DEPTH):
        proj = x @ w_in[l]
        a_q, a_kv, a_g, b_u, c_qkv, c_f = jnp.split(proj, PROJ_SPLITS, axis=-1)
        o_a = nsa_mixer(a_q, a_kv, a_g, w_cmp1[l], w_cmp2[l], pe_cmp[l])
        o_b = conv_mixer(b_u, w_dw[l], b_dw[l], ln_conv_g[l], ln_conv_b[l])
        o_c = fox_mixer(c_qkv, c_f, b_f[l])
        mix = jnp.concatenate([o_a, o_b.astype(o_a.dtype), o_c], axis=-1) @ w_out[l]
        x = layer_norm(ALPHA * x + mix, ln1_g[l], ln1_b[l])
        gate_pre, up = jnp.split(x @ w_ffn_in[l], 2, axis=-1)
        h = jax.nn.silu(causal_dwconv(gate_pre, w_ffn_conv[l], b_ffn_conv[l])) * up
        x = layer_norm(ALPHA * x + h @ w_ffn_down[l], ln2_g[l], ln2_b[l])
    return x
```

```python
import functools
import math

import jax
import jax.numpy as jnp
from jax import lax
from jax.experimental import pallas as pl
from jax.experimental.pallas import tpu as pltpu

HEAD_DIM = 64
NSA_HEADS = 6
NSA_KV_HEADS = 2
NSA_GROUP = 3
FOX_HEADS = 6
NSA_WIDTH = NSA_HEADS * HEAD_DIM
CONV_WIDTH = 256
FOX_WIDTH = FOX_HEADS * HEAD_DIM
CMP_BLOCK = 32
SEL_BLOCK = 64
SEL_TOPK = 16
WINDOW = 512
LN_EPS = 1e-5
NEG_INF = -1e30
SCALE = HEAD_DIM ** -0.5

LANES = 128
VMEM_LIMIT = 56 * 1024 * 1024

BF16 = jnp.bfloat16
F32 = jnp.float32
NT_DIMS = (((1,), (1,)), ((), ()))


def _cparams(*sem):
    return pltpu.CompilerParams(dimension_semantics=sem, vmem_limit_bytes=VMEM_LIMIT)


def _ln_rows(x, g, b):
    mu = jnp.mean(x, axis=-1, keepdims=True)
    xc = x - mu
    var = jnp.mean(xc * xc, axis=-1, keepdims=True)
    return xc * lax.rsqrt(var + LN_EPS) * g + b


def _ln_kernel(x_ref, g_ref, b_ref, o_ref):
    o_ref[...] = _ln_rows(x_ref[...], g_ref[...], b_ref[...])


def _layer_norm(x2d, g, b, tm=1024):
    t, d = x2d.shape
    return pl.pallas_call(
        _ln_kernel,
        grid=(t // tm,),
        in_specs=[pl.BlockSpec((tm, d), lambda i: (i, 0)),
                  pl.BlockSpec((1, d), lambda i: (0, 0)),
                  pl.BlockSpec((1, d), lambda i: (0, 0))],
        out_specs=pl.BlockSpec((tm, d), lambda i: (i, 0)),
        out_shape=jax.ShapeDtypeStruct((t, d), F32),
        compiler_params=_cparams("parallel"),
        name="ln_emb",
    )(x2d, g.reshape(1, d), b.reshape(1, d))


ROWS_SPLIT = (NSA_WIDTH, 128, 128, FOX_WIDTH, FOX_WIDTH)
F32_SPLIT = (256, 2 * CONV_WIDTH)
T_SPLIT = (128, 128, FOX_WIDTH, 24, 8)


def _proj_kernel(x_ref, wr_ref, wf_ref, wt_ref,
                 nq_ref, kslc_ref, kwin_ref, fq_ref, fk_ref, cmp_ref, conv_ref,
                 vslc_ref, vwin_ref, fv_ref, gate_ref, flog_ref):
    xb = x_ref[0].astype(BF16)
    rows = jnp.dot(xb, wr_ref[...], preferred_element_type=F32)
    off = 0
    for ref, w in zip((nq_ref, kslc_ref, kwin_ref, fq_ref, fk_ref), ROWS_SPLIT):
        ref[0] = rows[:, off:off + w].astype(ref.dtype)
        off += w
    f32o = jnp.dot(xb, wf_ref[...], preferred_element_type=F32)
    cmp_ref[0] = f32o[:, :F32_SPLIT[0]]
    conv_ref[0] = f32o[:, F32_SPLIT[0]:]
    tr = lax.dot_general(wt_ref[...], xb, NT_DIMS, preferred_element_type=F32)
    off = 0
    for ref, w in zip((vslc_ref, vwin_ref, fv_ref, gate_ref, flog_ref), T_SPLIT):
        ref[0] = tr[off:off + w, :].astype(ref.dtype)
        off += w


def _split_w_in(w_in_l):
    kvw = NSA_KV_HEADS * HEAD_DIM
    o = 0
    w_nq = w_in_l[:, o:o + NSA_WIDTH]; o += NSA_WIDTH
    w_kv = w_in_l[:, o:o + 6 * kvw]; o += 6 * kvw
    w_g = w_in_l[:, o:o + 3 * NSA_HEADS]; o += 3 * NSA_HEADS
    w_conv = w_in_l[:, o:o + 2 * CONV_WIDTH]; o += 2 * CONV_WIDTH
    w_fqkv = w_in_l[:, o:o + 3 * FOX_WIDTH]; o += 3 * FOX_WIDTH
    w_f = w_in_l[:, o:o + FOX_HEADS]
    d = w_in_l.shape[0]
    order = [g * NSA_GROUP + j for j in range(NSA_GROUP) for g in range(NSA_KV_HEADS)]
    w_nq = w_nq.reshape(d, NSA_HEADS, HEAD_DIM)[:, jnp.array(order), :].reshape(d, NSA_WIDTH) * SCALE
    kv = [w_kv[:, i * kvw:(i + 1) * kvw] for i in range(6)]
    w_fq = w_fqkv[:, :FOX_WIDTH] * SCALE
    w_fk = w_fqkv[:, FOX_WIDTH:2 * FOX_WIDTH]
    w_fv = w_fqkv[:, 2 * FOX_WIDTH:]
    w_rows = jnp.concatenate([w_nq, kv[2], kv[4], w_fq, w_fk], axis=1).astype(BF16)
    w_f32 = jnp.concatenate([kv[0], kv[1], w_conv], axis=1).astype(BF16)
    w_gp = jnp.pad(w_g, ((0, 0), (0, T_SPLIT[3] - w_g.shape[1])))
    w_fp = jnp.pad(w_f, ((0, 0), (0, T_SPLIT[4] - w_f.shape[1])))
    w_t = jnp.concatenate([kv[3], kv[5], w_fv, w_gp, w_fp], axis=1).T.astype(BF16)
    return w_rows, w_f32, w_t


def _project(x3d, w_rows, w_f32, w_t, tm=512):
    b, s, d = x3d.shape
    row_spec = lambda w: pl.BlockSpec((1, tm, w), lambda bi, i: (bi, i, 0))
    t_spec = lambda w: pl.BlockSpec((1, w, tm), lambda bi, i: (bi, 0, i))
    full = lambda a: pl.BlockSpec(a.shape, lambda bi, i: (0, 0))
    out_shape = ([jax.ShapeDtypeStruct((b, s, w), BF16) for w in ROWS_SPLIT]
                 + [jax.ShapeDtypeStruct((b, s, w), F32) for w in F32_SPLIT]
                 + [jax.ShapeDtypeStruct((b, w, s), dt) for w, dt in zip(T_SPLIT, (BF16, BF16, BF16, F32, F32))])
    out_specs = [row_spec(w) for w in ROWS_SPLIT + F32_SPLIT] + [t_spec(w) for w in T_SPLIT]
    return pl.pallas_call(
        _proj_kernel,
        grid=(b, s // tm),
        in_specs=[row_spec(d), full(w_rows), full(w_f32), full(w_t)],
        out_specs=out_specs,
        out_shape=out_shape,
        compiler_params=_cparams("parallel", "parallel"),
        name="in_proj",
    )(x3d, w_rows, w_f32, w_t)


def _compress_kernel(x_ref, pe_ref, w1_ref, w2_ref, o_ref):
    blk = (x_ref[0] + pe_ref[0]).astype(BF16)
    hid = jax.nn.gelu(jnp.dot(blk, w1_ref[0], preferred_element_type=F32))
    o_ref[0] = jnp.dot(hid.astype(BF16), w2_ref[0], preferred_element_type=F32)


def _compress(xblk, pe_flat, w1, w2):
    _, r, f = xblk.shape
    tr = min(512, r)
    hid = w1.shape[-1]
    return pl.pallas_call(
        _compress_kernel,
        grid=(2, r // tr),
        in_specs=[pl.BlockSpec((1, tr, f), lambda i, j: (i, j, 0)),
                  pl.BlockSpec((1, 1, f), lambda i, j: (i, 0, 0)),
                  pl.BlockSpec((1, f, hid), lambda i, j: (i, 0, 0)),
                  pl.BlockSpec((1, hid, HEAD_DIM), lambda i, j: (i, 0, 0))],
        out_specs=pl.BlockSpec((1, tr, HEAD_DIM), lambda i, j: (i, j, 0)),
        out_shape=jax.ShapeDtypeStruct((2, r, HEAD_DIM), F32),
        compiler_params=_cparams("parallel", "parallel"),
        name="nsa_compress",
    )(xblk, pe_flat, w1.astype(BF16), w2.astype(BF16))


NSA_TQ = 128
NSA_KT = 512


def _alibi_slope(h):
    return float(2.0 ** (-8.0 * (h + 1) / NSA_HEADS))


def _nsa_kernel(q_ref, kc_ref, vct_ref, kslc_ref, vslct_ref, kwin_ref, vwint_ref, gate_ref, e_ref, o_ref,
                *, seq_len):
    tq, kt = NSA_TQ, NSA_KT
    n_cmp = seq_len // CMP_BLOCK
    n_sel = seq_len // SEL_BLOCK
    top_k = min(SEL_TOPK, n_sel)
    kw = WINDOW + tq
    qi = pl.program_id(1)
    q0 = qi * tq
    t_row = q0 + lax.broadcasted_iota(jnp.int32, (1, tq), 1)
    half = lax.broadcasted_iota(jnp.int32, (tq, LANES), 1) // HEAD_DIM

    def softmax_pv(s, valid, vt):
        ps, outs = [], []
        for j in range(NSA_GROUP):
            sj = jnp.where(valid, s[:, j * tq:(j + 1) * tq], NEG_INF)
            m = jnp.max(sj, axis=0, keepdims=True)
            p = jnp.where(valid, jnp.exp(sj - m), 0.0)
            l = jnp.sum(p, axis=0, keepdims=True)
            ps.append(p / jnp.maximum(l, 1e-30))
        p_all = jnp.concatenate(ps, axis=1)
        return p_all, jnp.dot(vt, p_all.astype(BF16), preferred_element_type=F32)

    out_rows = [None] * NSA_HEADS
    for g in range(NSA_KV_HEADS):
        qs = jnp.concatenate(
            [jnp.where(half == g, q_ref[0, :, j * LANES:(j + 1) * LANES], 0).astype(BF16) for j in range(NSA_GROUP)],
            axis=0)
        slopes = [_alibi_slope(g * NSA_GROUP + j) for j in range(NSA_GROUP)]

        def biased(s, dist_f):
            return jnp.concatenate([s[:, j * tq:(j + 1) * tq] - slopes[j] * dist_f for j in range(NSA_GROUP)], axis=1)

        s_c = lax.dot_general(kc_ref[0], qs, NT_DIMS, preferred_element_type=F32)
        r = lax.broadcasted_iota(jnp.int32, (n_cmp, tq), 0)
        blk_id = jnp.where(r < n_sel, 2 * r, 2 * (r - n_sel) + 1)
        dist_c = t_row - (blk_id * CMP_BLOCK + CMP_BLOCK - 1)
        p_c, o_c = softmax_pv(biased(s_c, dist_c.astype(F32)), dist_c >= 0,
                              vct_ref[0, g * HEAD_DIM:(g + 1) * HEAD_DIM, :])

        imp = p_c[:, 0:tq] + p_c[:, tq:2 * tq] + p_c[:, 2 * tq:3 * tq]
        imp = imp[0:n_sel, :] + imp[n_sel:2 * n_sel, :]
        jid = lax.broadcasted_iota(jnp.int32, (n_sel, tq), 0)
        cur = t_row // SEL_BLOCK
        forced = (jid == 0) | (jid == cur)
        score = jnp.where(jid > cur, -1.0, jnp.where(forced, NSA_GROUP + 1.0, imp))
        sel = jnp.zeros((n_sel, tq), F32)
        for _ in range(top_k):
            mx = jnp.max(score, axis=0, keepdims=True)
            first = jnp.min(jnp.where(score == mx, jid, n_sel), axis=0, keepdims=True)
            hit = jid == first
            sel = jnp.where(hit, 1.0, sel)
            score = jnp.where(hit, -2.0, score)
        sel_b = sel.astype(BF16)

        def sel_tile(jt, carry):
            m_i, l_i, acc = carry
            k0 = pl.multiple_of(jt * kt, kt)
            s = lax.dot_general(kslc_ref[0, pl.ds(k0, kt), :], qs, NT_DIMS, preferred_element_type=F32)
            picked = jnp.dot(e_ref[pl.ds(k0, kt), :], sel_b, preferred_element_type=F32)
            dist = t_row - (k0 + lax.broadcasted_iota(jnp.int32, (kt, tq), 0))
            valid = (picked > 0.5) & (dist >= 0)
            sb = biased(s, dist.astype(F32))
            ms, ls, ps = [], [], []
            for j in range(NSA_GROUP):
                sl = slice(j * tq, (j + 1) * tq)
                sj = jnp.where(valid, sb[:, sl], NEG_INF)
                m_new = jnp.maximum(m_i[:, sl], jnp.max(sj, axis=0, keepdims=True))
                p = jnp.where(valid, jnp.exp(sj - m_new), 0.0)
                alpha = jnp.exp(m_i[:, sl] - m_new)
                ms.append(m_new)
                ls.append(alpha * l_i[:, sl] + jnp.sum(p, axis=0, keepdims=True))
                ps.append(p.astype(BF16))
            m_new = jnp.concatenate(ms, axis=1)
            alpha = jnp.exp(m_i - m_new)
            pv = jnp.dot(vslct_ref[0, g * HEAD_DIM:(g + 1) * HEAD_DIM, pl.ds(k0, kt)],
                         jnp.concatenate(ps, axis=1), preferred_element_type=F32)
            return m_new, jnp.concatenate(ls, axis=1), alpha * acc + pv

        n_tiles = (q0 + tq - 1) // kt + 1
        init = (jnp.full((1, NSA_GROUP * tq), NEG_INF, F32), jnp.zeros((1, NSA_GROUP * tq), F32),
                jnp.zeros((HEAD_DIM, NSA_GROUP * tq), F32))
        _, l_s, acc_s = lax.fori_loop(0, n_tiles, sel_tile, init)
        o_s = acc_s / jnp.maximum(l_s, 1e-30)

        w0 = pl.multiple_of(jnp.maximum(q0 - WINDOW, 0), tq)
        s_w = lax.dot_general(kwin_ref[0, pl.ds(w0, kw), :], qs, NT_DIMS, preferred_element_type=F32)
        dist_w = t_row - (w0 + lax.broadcasted_iota(jnp.int32, (kw, tq), 0))
        _, o_w = softmax_pv(biased(s_w, dist_w.astype(F32)), (dist_w >= 0) & (dist_w < WINDOW),
                            vwint_ref[0, g * HEAD_DIM:(g + 1) * HEAD_DIM, pl.ds(w0, kw)])

        for j in range(NSA_GROUP):
            h = g * NSA_GROUP + j
            gt = jax.nn.sigmoid(gate_ref[0, 3 * h:3 * h + 3, :])
            sl = slice(j * tq, (j + 1) * tq)
            out_rows[h] = gt[0:1] * o_c[:, sl] + gt[1:2] * o_s[:, sl] + gt[2:3] * o_w[:, sl]

    o_ref[0] = jnp.concatenate(out_rows, axis=0).T.astype(o_ref.dtype)


def _nsa_attention(nq, kc, vct, kslc, vslct, kwin, vwint, gates_t, e_sel):
    b, s, _ = nq.shape
    n_cmp = s // CMP_BLOCK
    per_b = lambda shape: pl.BlockSpec((1,) + shape, lambda bi, i: (bi, 0, 0))
    return pl.pallas_call(
        functools.partial(_nsa_kernel, seq_len=s),
        grid=(b, s // NSA_TQ),
        in_specs=[pl.BlockSpec((1, NSA_TQ, NSA_WIDTH), lambda bi, i: (bi, i, 0)),
                  per_b((n_cmp, LANES)), per_b((LANES, n_cmp)),
                  per_b((s, LANES)), per_b((LANES, s)),
                  per_b((s, LANES)), per_b((LANES, s)),
                  pl.BlockSpec((1, gates_t.shape[1], NSA_TQ), lambda bi, i: (bi, 0, i)),
                  pl.BlockSpec(e_sel.shape, lambda bi, i: (0, 0))],
        out_specs=pl.BlockSpec((1, NSA_TQ, NSA_WIDTH), lambda bi, i: (bi, i, 0)),
        out_shape=jax.ShapeDtypeStruct((b, s, NSA_WIDTH), BF16),
        compiler_params=_cparams("parallel", "arbitrary"),
        name="nsa_attention",
    )(nq, kc, vct, kslc, vslct, kwin, vwint, gates_t, e_sel)


def _cumsum_kernel(f_ref, b_ref, c_ref):
    s = f_ref.shape[-1]
    ri = lax.broadcasted_iota(jnp.int32, (LANES, LANES), 0)
    ci = lax.broadcasted_iota(jnp.int32, (LANES, LANES), 1)
    upper = (ri <= ci).astype(F32)
    ones = jnp.ones((LANES, LANES), F32)

    def chunk(c, carry):
        c0 = pl.multiple_of(c * LANES, LANES)
        lf = jax.nn.log_sigmoid(f_ref[0, :, pl.ds(c0, LANES)] + b_ref[...])
        within = jnp.dot(lf, upper, preferred_element_type=F32, precision=lax.Precision.HIGHEST)
        total = jnp.dot(lf, ones, preferred_element_type=F32, precision=lax.Precision.HIGHEST)
        c_ref[0, :, pl.ds(c0, LANES)] = within + carry
        return carry + total

    lax.fori_loop(0, s // LANES, chunk, jnp.zeros((f_ref.shape[1], LANES), F32))


def _fox_cumsum(f_t, b_f):
    b, r, s = f_t.shape
    bias = jnp.broadcast_to(jnp.pad(b_f, (0, r - b_f.shape[0]))[:, None], (r, LANES)).astype(F32)
    return pl.pallas_call(
        _cumsum_kernel,
        grid=(b,),
        in_specs=[pl.BlockSpec((1, r, s), lambda i: (i, 0, 0)),
                  pl.BlockSpec((r, LANES), lambda i: (0, 0))],
        out_specs=pl.BlockSpec((1, r, s), lambda i: (i, 0, 0)),
        out_shape=jax.ShapeDtypeStruct((b, r, s), F32),
        compiler_params=_cparams("parallel"),
        name="fox_cumsum",
    )(f_t, bias)


FOX_TQ = 256
FOX_KT = 512


def _fox_kernel(q_ref, k_ref, vt_ref, cq_ref, ck_ref, o_ref):
    tq, kt = FOX_TQ, FOX_KT
    qi = pl.program_id(2)
    q0 = qi * tq
    half = lax.broadcasted_iota(jnp.int32, (tq, LANES), 1) // HEAD_DIM
    t_row = q0 + lax.broadcasted_iota(jnp.int32, (1, tq), 1)
    n_full = q0 // kt
    outs = []
    for hh in range(2):
        qh = jnp.where(half == hh, q_ref[0], 0).astype(BF16)
        cq = cq_ref[0, hh]

        def tile(jt, carry, masked):
            m_i, l_i, acc = carry
            k0 = pl.multiple_of(jt * kt, kt)
            s = lax.dot_general(k_ref[0, pl.ds(k0, kt), :], qh, NT_DIMS, preferred_element_type=F32)
            ck = ck_ref[0, hh, pl.ds(k0, kt), :]
            s = s + (cq - jnp.concatenate([ck] * (tq // LANES), axis=1))
            if masked:
                valid = t_row >= k0 + lax.broadcasted_iota(jnp.int32, (kt, tq), 0)
                s = jnp.where(valid, s, NEG_INF)
            m_new = jnp.maximum(m_i, jnp.max(s, axis=0, keepdims=True))
            p = jnp.exp(s - m_new)
            if masked:
                p = jnp.where(valid, p, 0.0)
            alpha = jnp.exp(m_i - m_new)
            l_new = alpha * l_i + jnp.sum(p, axis=0, keepdims=True)
            pv = jnp.dot(vt_ref[0, hh * HEAD_DIM:(hh + 1) * HEAD_DIM, pl.ds(k0, kt)], p.astype(BF16),
                         preferred_element_type=F32)
            return m_new, l_new, alpha * acc + pv

        init = (jnp.full((1, tq), NEG_INF, F32), jnp.zeros((1, tq), F32), jnp.zeros((HEAD_DIM, tq), F32))
        carry = lax.fori_loop(0, n_full, functools.partial(tile, masked=False), init)
        _, l_f, acc = tile(n_full, carry, True)
        outs.append(acc / l_f)
    o_ref[0] = jnp.concatenate(outs, axis=0).T.astype(o_ref.dtype)


def _fox_attention(fq, fk, fvt, c_rows, ck_rep):
    b, s, _ = fq.shape
    n_pair = FOX_HEADS // 2
    return pl.pallas_call(
        _fox_kernel,
        grid=(b, n_pair, s // FOX_TQ),
        in_specs=[pl.BlockSpec((1, FOX_TQ, LANES), lambda bi, p, i: (bi, i, p)),
                  pl.BlockSpec((1, s, LANES), lambda bi, p, i: (bi, 0, p)),
                  pl.BlockSpec((1, LANES, s), lambda bi, p, i: (bi, p, 0)),
                  pl.BlockSpec((1, 2, 1, FOX_TQ), lambda bi, p, i: (bi, p, 0, i)),
                  pl.BlockSpec((1, 2, s, LANES), lambda bi, p, i: (bi, p, 0, 0))],
        out_specs=pl.BlockSpec((1, FOX_TQ, LANES), lambda bi, p, i: (bi, i, p)),
        out_shape=jax.ShapeDtypeStruct((b, s, FOX_WIDTH), BF16),
        compiler_params=_cparams("parallel", "parallel", "arbitrary"),
        name="fox_attention",
    )(fq, fk, fvt, c_rows, ck_rep)


CONV_TS = 512
CONV_HALO = 32


def _conv_kernel(u_ref, uh_ref, w_ref, b_ref, g_ref, be_ref, o_ref, y_ref, *, ksize):
    ts, halo, cw = CONV_TS, CONV_HALO, CONV_WIDTH

    def glu(u):
        return u[:, :cw] * jax.nn.sigmoid(u[:, cw:])

    first = pl.program_id(1) == 0
    y_ref[0:halo, :] = jnp.where(first, 0.0, glu(uh_ref[0]))
    y_ref[halo:halo + ts, :] = glu(u_ref[0])
    acc = jnp.zeros((ts, cw), F32)
    for k in range(ksize):
        start = halo - (ksize - 1) + k
        acc = acc + w_ref[k:k + 1, :] * y_ref[start:start + ts, :]
    hn = _ln_rows(acc + b_ref[...], g_ref[...], be_ref[...])
    o_ref[0] = jax.nn.silu(hn).astype(o_ref.dtype)


def _conv_mixer(u, w_dw, b_dw, ln_g, ln_b):
    b, s, _ = u.shape
    ksize = w_dw.shape[0]
    assert ksize - 1 <= CONV_HALO
    ratio = CONV_TS // CONV_HALO
    vec = lambda a: a.reshape(1, CONV_WIDTH)
    const = lambda shape: pl.BlockSpec(shape, lambda bi, i: (0, 0))
    return pl.pallas_call(
        functools.partial(_conv_kernel, ksize=ksize),
        grid=(b, s // CONV_TS),
        in_specs=[pl.BlockSpec((1, CONV_TS, 2 * CONV_WIDTH), lambda bi, i: (bi, i, 0)),
                  pl.BlockSpec((1, CONV_HALO, 2 * CONV_WIDTH), lambda bi, i: (bi, jnp.maximum(i * ratio - 1, 0), 0)),
                  const((ksize, CONV_WIDTH)), const((1, CONV_WIDTH)), const((1, CONV_WIDTH)),
                  const((1, CONV_WIDTH))],
        out_specs=pl.BlockSpec((1, CONV_TS, CONV_WIDTH), lambda bi, i: (bi, i, 0)),
        out_shape=jax.ShapeDtypeStruct((b, s, CONV_WIDTH), BF16),
        scratch_shapes=[pltpu.VMEM((CONV_HALO + CONV_TS, CONV_WIDTH), F32)],
        compiler_params=_cparams("parallel", "arbitrary"),
        name="conv_mixer",
    )(u, u, w_dw, vec(b_dw), vec(ln_g), vec(ln_b))


def _outproj_kernel(oa_ref, ob_ref, oc_ref, x_ref, wa_ref, wb_ref, wc_ref, g_ref, b_ref, o_ref, *, alpha):
    mix = jnp.dot(oa_ref[...], wa_ref[...], preferred_element_type=F32)
    mix = mix + jnp.dot(ob_ref[...], wb_ref[...], preferred_element_type=F32)
    mix = mix + jnp.dot(oc_ref[...], wc_ref[...], preferred_element_type=F32)
    o_ref[...] = _ln_rows(alpha * x_ref[...] + mix, g_ref[...], b_ref[...])


def _out_proj(o_a, o_b, o_c, x2d, w_out, g, b, alpha, tm=512):
    t, d = x2d.shape
    wa = w_out[:NSA_WIDTH].astype(BF16)
    wb = w_out[NSA_WIDTH:NSA_WIDTH + CONV_WIDTH].astype(BF16)
    wc = w_out[NSA_WIDTH + CONV_WIDTH:].astype(BF16)
    rows = lambda w: pl.BlockSpec((tm, w), lambda i: (i, 0))
    full = lambda a: pl.BlockSpec(a.shape, lambda i: (0, 0))
    return pl.pallas_call(
        functools.partial(_outproj_kernel, alpha=alpha),
        grid=(t // tm,),
        in_specs=[rows(NSA_WIDTH), rows(CONV_WIDTH), rows(FOX_WIDTH), rows(d), full(wa), full(wb), full(wc),
                  pl.BlockSpec((1, d), lambda i: (0, 0)), pl.BlockSpec((1, d), lambda i: (0, 0))],
        out_specs=rows(d),
        out_shape=jax.ShapeDtypeStruct((t, d), F32),
        compiler_params=_cparams("parallel"),
        name="out_proj_ln",
    )(o_a, o_b, o_c, x2d, wa, wb, wc, g.reshape(1, d), b.reshape(1, d))


FFN_TM = 512
FFN_HALO = 8
FFN_FC = 256


def _ffn_kernel(x_ref, xh_ref, wg_ref, wu_ref, wd_ref, cw_ref, cb_ref, g_ref, b_ref, o_ref, gate_ref,
                *, alpha, n_chunks, ksize):
    tm, halo, fc = FFN_TM, FFN_HALO, FFN_FC
    x = x_ref[0]
    xb = x.astype(BF16)
    xhb = xh_ref[0].astype(BF16)
    first = pl.program_id(1) == 0
    acc = jnp.zeros((tm, x.shape[-1]), F32)
    for c in range(n_chunks):
        cs = slice(c * fc, (c + 1) * fc)
        gate_ref[0:halo, :] = jnp.where(first, 0.0, jnp.dot(xhb, wg_ref[:, cs], preferred_element_type=F32))
        gate_ref[halo:halo + tm, :] = jnp.dot(xb, wg_ref[:, cs], preferred_element_type=F32)
        conv = cb_ref[:, cs]
        for k in range(ksize):
            start = halo - (ksize - 1) + k
            conv = conv + cw_ref[k:k + 1, cs] * gate_ref[start:start + tm, :]
        up = jnp.dot(xb, wu_ref[:, cs], preferred_element_type=F32)
        h = (jax.nn.silu(conv) * up).astype(BF16)
        acc = acc + jnp.dot(h, wd_ref[cs, :], preferred_element_type=F32)
    o_ref[0] = _ln_rows(alpha * x + acc, g_ref[...], b_ref[...])


def _conv_ffn(x3d, w_ffn_in, w_conv, b_conv, w_down, g, b, alpha):
    bsz, s, d = x3d.shape
    d_ff = w_down.shape[0]
    ksize = w_conv.shape[0]
    assert d_ff % FFN_FC == 0 and ksize - 1 <= FFN_HALO
    wg = w_ffn_in[:, :d_ff].astype(BF16)
    wu = w_ffn_in[:, d_ff:].astype(BF16)
    wd = w_down.astype(BF16)
    ratio = FFN_TM // FFN_HALO
    const = lambda shape: pl.BlockSpec(shape, lambda bi, i: (0, 0))
    return pl.pallas_call(
        functools.partial(_ffn_kernel, alpha=alpha, n_chunks=d_ff // FFN_FC, ksize=ksize),
        grid=(bsz, s // FFN_TM),
        in_specs=[pl.BlockSpec((1, FFN_TM, d), lambda bi, i: (bi, i, 0)),
                  pl.BlockSpec((1, FFN_HALO, d), lambda bi, i: (bi, jnp.maximum(i * ratio - 1, 0), 0)),
                  const((d, d_ff)), const((d, d_ff)), const((d_ff, d)),
                  const((ksize, d_ff)), const((1, d_ff)), const((1, d)), const((1, d))],
        out_specs=pl.BlockSpec((1, FFN_TM, d), lambda bi, i: (bi, i, 0)),
        out_shape=jax.ShapeDtypeStruct((bsz, s, d), F32),
        scratch_shapes=[pltpu.VMEM((FFN_HALO + FFN_TM, FFN_FC), F32)],
        compiler_params=_cparams("parallel", "arbitrary"),
        name="conv_ffn_ln",
    )(x3d, x3d, wg, wu, wd, w_conv, b_conv.reshape(1, d_ff), g.reshape(1, d), b.reshape(1, d))


def _cmp_blocks(raw, bsz, s):
    n_cmp = s // CMP_BLOCK
    t = raw.reshape(bsz, n_cmp, CMP_BLOCK, NSA_KV_HEADS, HEAD_DIM).transpose(0, 3, 1, 2, 4)
    return t.reshape(bsz * NSA_KV_HEADS * n_cmp, CMP_BLOCK * HEAD_DIM)


def _cmp_even_odd(c, bsz, s):
    n_cmp = s // CMP_BLOCK
    c = c.reshape(bsz, NSA_KV_HEADS, n_cmp // 2, 2, HEAD_DIM).transpose(0, 3, 2, 1, 4)
    return c.reshape(bsz, n_cmp, NSA_KV_HEADS * HEAD_DIM)


def kernel(x, ln_emb_g, ln_emb_b, w_in, b_f, w_cmp1, w_cmp2, pe_cmp, w_dw, b_dw, ln_conv_g, ln_conv_b, w_out,
           ln1_g, ln1_b, w_ffn_in, w_ffn_conv, b_ffn_conv, w_ffn_down, ln2_g, ln2_b):
    bsz, s, d = x.shape
    depth = w_in.shape[0]
    alpha = (2.0 * depth) ** 0.25
    t = bsz * s
    n_sel = s // SEL_BLOCK
    e_sel = (jnp.arange(s)[:, None] // SEL_BLOCK == jnp.arange(n_sel)[None, :]).astype(BF16)

    xc = _layer_norm(x.reshape(t, d), ln_emb_g, ln_emb_b)
    for l in range(depth):
        (nq, kslc, kwin, fq, fk, cmp_raw, conv_u, vslct, vwint, fvt, gates_t, f_t) = _project(
            xc.reshape(bsz, s, d), *_split_w_in(w_in[l]))

        kvw = NSA_KV_HEADS * HEAD_DIM
        xblk = jnp.stack([_cmp_blocks(cmp_raw[..., :kvw], bsz, s), _cmp_blocks(cmp_raw[..., kvw:], bsz, s)])
        cmp_out = _compress(xblk, pe_cmp[l].reshape(2, 1, CMP_BLOCK * HEAD_DIM), w_cmp1[l], w_cmp2[l])
        kc = _cmp_even_odd(cmp_out[0], bsz, s).astype(BF16)
        vct = _cmp_even_odd(cmp_out[1], bsz, s).astype(BF16).transpose(0, 2, 1)
        o_a = _nsa_attention(nq, kc, vct, kslc, vslct, kwin, vwint, gates_t, e_sel)

        o_b = _conv_mixer(conv_u, w_dw[l], b_dw[l], ln_conv_g[l], ln_conv_b[l])

        c = _fox_cumsum(f_t, b_f[l])
        c6 = c[:, :FOX_HEADS]
        c_rows = c6.reshape(bsz, FOX_HEADS, 1, s)
        ck_rep = jnp.broadcast_to(c6[..., None], (bsz, FOX_HEADS, s, LANES))
        o_c = _fox_attention(fq, fk, fvt, c_rows, ck_rep)

        xc = _out_proj(o_a.reshape(t, NSA_WIDTH), o_b.reshape(t, CONV_WIDTH), o_c.reshape(t, FOX_WIDTH),
                       xc, w_out[l], ln1_g[l], ln1_b[l], alpha)
        xc = _conv_ffn(xc.reshape(bsz, s, d), w_ffn_in[l], w_ffn_conv[l], b_ffn_conv[l], w_ffn_down[l],
                       ln2_g[l], ln2_b[l], alpha).reshape(t, d)
    return xc.reshape(bsz, s, d)
```

```python
import functools

import jax
import jax.numpy as jnp
import numpy as np
from jax import lax
from jax.experimental import pallas as pl
from jax.experimental.pallas import tpu as pltpu

HEAD_DIM = 64
NSA_HEADS = 6
NSA_KV_HEADS = 2
NSA_GROUP = 3
FOX_HEADS = 6
NSA_WIDTH = NSA_HEADS * HEAD_DIM
CONV_WIDTH = 256
FOX_WIDTH = FOX_HEADS * HEAD_DIM
CMP_BLOCK = 32
SEL_BLOCK = 64
SEL_TOPK = 16
WINDOW = 512
LN_EPS = 1e-5
NEG_INF = -1e30
SCALE = HEAD_DIM ** -0.5
LOG2E = 1.4426950408889634
BIAS_PIECES = 3
BIAS_LANE0 = HEAD_DIM
MASK_BIG = 2.0 ** 100

LANES = 128
VMEM_LIMIT = 56 * 1024 * 1024

BF16 = jnp.bfloat16
F32 = jnp.float32
NT_DIMS = (((1,), (1,)), ((), ()))


def _cparams(*sem):
    return pltpu.CompilerParams(dimension_semantics=sem, vmem_limit_bytes=VMEM_LIMIT)


def _ln_rows(x, g, b):
    mu = jnp.mean(x, axis=-1, keepdims=True)
    xc = x - mu
    var = jnp.mean(xc * xc, axis=-1, keepdims=True)
    return xc * lax.rsqrt(var + LN_EPS) * g + b


def _bf16_pieces(value, n=BIAS_PIECES):
    out = []
    rest = np.float32(value)
    for _ in range(n):
        bits = np.float32(rest).view(np.uint32)
        bits = (bits + np.uint32(0x7FFF) + ((bits >> np.uint32(16)) & np.uint32(1))) & np.uint32(0xFFFF0000)
        piece = bits.view(np.float32)
        out.append(float(piece))
        rest = np.float32(rest - piece)
    return out


def _position_lanes(pos):
    hi = ((pos // SEL_BLOCK) * SEL_BLOCK).astype(F32)
    lo = (pos % SEL_BLOCK).astype(F32)
    lanes = jnp.zeros((pos.shape[0], LANES), F32)
    for i in range(BIAS_PIECES):
        lanes = lanes.at[:, BIAS_LANE0 + 2 * i].set(hi)
        lanes = lanes.at[:, BIAS_LANE0 + 2 * i + 1].set(lo)
    return lanes


def _ln_kernel(x_ref, g_ref, b_ref, o_ref):
    o_ref[...] = _ln_rows(x_ref[...], g_ref[...], b_ref[...])


def _layer_norm(x2d, g, b, tm=1024):
    t, d = x2d.shape
    return pl.pallas_call(
        _ln_kernel,
        grid=(t // tm,),
        in_specs=[pl.BlockSpec((tm, d), lambda i: (i, 0)),
                  pl.BlockSpec((1, d), lambda i: (0, 0)),
                  pl.BlockSpec((1, d), lambda i: (0, 0))],
        out_specs=pl.BlockSpec((tm, d), lambda i: (i, 0)),
        out_shape=jax.ShapeDtypeStruct((t, d), F32),
        compiler_params=_cparams("parallel"),
        name="ln_emb",
    )(x2d, g.reshape(1, d), b.reshape(1, d))


NSA_Q_COLS = NSA_HEADS * LANES
NSA_K_COLS = NSA_KV_HEADS * LANES
ROWS_SPLIT = (NSA_Q_COLS, NSA_K_COLS, NSA_K_COLS, FOX_WIDTH, FOX_WIDTH)
F32_SPLIT = (256, 2 * CONV_WIDTH)
T_SPLIT = (128, 128, FOX_WIDTH, 24, 8)


def _proj_kernel(x_ref, wr_ref, wf_ref, wt_ref, pos_ref,
                 nq_ref, kslc_ref, kwin_ref, fq_ref, fk_ref, cmp_ref, conv_ref,
                 vslc_ref, vwin_ref, fv_ref, gate_ref, flog_ref):
    xb = x_ref[0].astype(BF16)
    rows = jnp.dot(xb, wr_ref[...], preferred_element_type=F32)
    off = 0
    for ref, w in zip((nq_ref, kslc_ref, kwin_ref, fq_ref, fk_ref), ROWS_SPLIT):
        blk = rows[:, off:off + w]
        if ref is kslc_ref or ref is kwin_ref:
            blk = blk + pos_ref[...]
        ref[0] = blk.astype(ref.dtype)
        off += w
    f32o = jnp.dot(xb, wf_ref[...], preferred_element_type=F32)
    cmp_ref[0] = f32o[:, :F32_SPLIT[0]]
    conv_ref[0] = f32o[:, F32_SPLIT[0]:]
    tr = lax.dot_general(wt_ref[...], xb, NT_DIMS, preferred_element_type=F32)
    off = 0
    for ref, w in zip((vslc_ref, vwin_ref, fv_ref, gate_ref, flog_ref), T_SPLIT):
        ref[0] = tr[off:off + w, :].astype(ref.dtype)
        off += w


def _split_w_in(w_in_l):
    kvw = NSA_KV_HEADS * HEAD_DIM
    o = 0
    w_nq = w_in_l[:, o:o + NSA_WIDTH]; o += NSA_WIDTH
    w_kv = w_in_l[:, o:o + 6 * kvw]; o += 6 * kvw
    w_g = w_in_l[:, o:o + 3 * NSA_HEADS]; o += 3 * NSA_HEADS
    w_conv = w_in_l[:, o:o + 2 * CONV_WIDTH]; o += 2 * CONV_WIDTH
    w_fqkv = w_in_l[:, o:o + 3 * FOX_WIDTH]; o += 3 * FOX_WIDTH
    w_f = w_in_l[:, o:o + FOX_HEADS]
    d = w_in_l.shape[0]

    def head_blocks(w, n_heads):
        w = w.reshape(d, n_heads, HEAD_DIM)
        return jnp.pad(w, ((0, 0), (0, 0), (0, LANES - HEAD_DIM))).reshape(d, n_heads * LANES)

    w_nq = head_blocks(w_nq * (SCALE * LOG2E), NSA_HEADS)
    kv = [w_kv[:, i * kvw:(i + 1) * kvw] for i in range(6)]
    w_fq = w_fqkv[:, :FOX_WIDTH] * (SCALE * LOG2E)
    w_fk = w_fqkv[:, FOX_WIDTH:2 * FOX_WIDTH]
    w_fv = w_fqkv[:, 2 * FOX_WIDTH:]
    w_rows = jnp.concatenate([w_nq, head_blocks(kv[2], NSA_KV_HEADS), head_blocks(kv[4], NSA_KV_HEADS),
                              w_fq, w_fk], axis=1).astype(BF16)
    w_f32 = jnp.concatenate([kv[0], kv[1], w_conv], axis=1).astype(BF16)
    w_gp = jnp.pad(w_g, ((0, 0), (0, T_SPLIT[3] - w_g.shape[1])))
    w_fp = jnp.pad(w_f, ((0, 0), (0, T_SPLIT[4] - w_f.shape[1])))
    w_t = jnp.concatenate([kv[3], kv[5], w_fv, w_gp, w_fp], axis=1).T.astype(BF16)
    return w_rows, w_f32, w_t


def _project(x3d, w_rows, w_f32, w_t, tm=512):
    b, s, d = x3d.shape
    pos_lanes = jnp.tile(_position_lanes(jnp.arange(s)), (1, NSA_KV_HEADS))
    row_spec = lambda w: pl.BlockSpec((1, tm, w), lambda bi, i: (bi, i, 0))
    t_spec = lambda w: pl.BlockSpec((1, w, tm), lambda bi, i: (bi, 0, i))
    full = lambda a: pl.BlockSpec(a.shape, lambda bi, i: (0, 0))
    out_shape = ([jax.ShapeDtypeStruct((b, s, w), BF16) for w in ROWS_SPLIT]
                 + [jax.ShapeDtypeStruct((b, s, w), F32) for w in F32_SPLIT]
                 + [jax.ShapeDtypeStruct((b, w, s), dt) for w, dt in zip(T_SPLIT, (BF16, BF16, BF16, F32, F32))])
    out_specs = [row_spec(w) for w in ROWS_SPLIT + F32_SPLIT] + [t_spec(w) for w in T_SPLIT]
    return pl.pallas_call(
        _proj_kernel,
        grid=(b, s // tm),
        in_specs=[row_spec(d), full(w_rows), full(w_f32), full(w_t),
                  pl.BlockSpec((tm, NSA_K_COLS), lambda bi, i: (i, 0))],
        out_specs=out_specs,
        out_shape=out_shape,
        compiler_params=_cparams("parallel", "parallel"),
        name="in_proj",
    )(x3d, w_rows, w_f32, w_t, pos_lanes)


def _compress_kernel(x_ref, pe_ref, w1_ref, w2_ref, o_ref):
    blk = (x_ref[0] + pe_ref[0]).astype(BF16)
    hid = jax.nn.gelu(jnp.dot(blk, w1_ref[0], preferred_element_type=F32))
    o_ref[0] = jnp.dot(hid.astype(BF16), w2_ref[0], preferred_element_type=F32)


def _compress(xblk, pe_flat, w1, w2):
    _, r, f = xblk.shape
    tr = min(512, r)
    hid = w1.shape[-1]
    return pl.pallas_call(
        _compress_kernel,
        grid=(2, r // tr),
        in_specs=[pl.BlockSpec((1, tr, f), lambda i, j: (i, j, 0)),
                  pl.BlockSpec((1, 1, f), lambda i, j: (i, 0, 0)),
                  pl.BlockSpec((1, f, hid), lambda i, j: (i, 0, 0)),
                  pl.BlockSpec((1, hid, HEAD_DIM), lambda i, j: (i, 0, 0))],
        out_specs=pl.BlockSpec((1, tr, HEAD_DIM), lambda i, j: (i, j, 0)),
        out_shape=jax.ShapeDtypeStruct((2, r, HEAD_DIM), F32),
        compiler_params=_cparams("parallel", "parallel"),
        name="nsa_compress",
    )(xblk, pe_flat, w1.astype(BF16), w2.astype(BF16))


NSA_TQ = 128
NSA_KT = 512


def _alibi_slope(h):
    return float(2.0 ** (-8.0 * (h + 1) / NSA_HEADS))


def _nsa_kernel(q_ref, kc_ref, vct_ref, kslc_ref, vslct_ref, kwin_ref, vwint_ref, gate_ref, e_ref, o_ref,
                s_buf, p_buf, idx_ref, *, seq_len):
    tq, kt = NSA_TQ, NSA_KT
    n_cmp = seq_len // CMP_BLOCK
    n_sel = seq_len // SEL_BLOCK
    n_tiles = seq_len // kt
    top_k = min(SEL_TOPK, n_sel)
    kw = WINDOW + tq
    wide = NSA_GROUP * tq
    qi = pl.program_id(1)
    q0 = qi * tq
    t_row = q0 + lax.broadcasted_iota(jnp.int32, (1, tq), 1)
    t_f = t_row.astype(F32)
    lane = lax.broadcasted_iota(jnp.int32, (1, LANES), 1)
    cur = t_row // SEL_BLOCK
    groups = range(NSA_KV_HEADS)

    def tile3(a):
        return jnp.concatenate([a] * NSA_GROUP, axis=1)

    def sum8(p):
        return jnp.sum(p.reshape(p.shape[0] // 8, 8, p.shape[1]), axis=0)

    q1, cq = [], []
    for g in groups:
        rows, consts = [], []
        for j in range(NSA_GROUP):
            h = g * NSA_GROUP + j
            slope2 = _alibi_slope(h) * LOG2E
            sl = jnp.zeros((1, LANES), F32)
            for i, piece in enumerate(_bf16_pieces(slope2)):
                sl = jnp.where((lane == BIAS_LANE0 + 2 * i) | (lane == BIAS_LANE0 + 2 * i + 1), piece, sl)
            rows.append(q_ref[0, :, h * LANES:(h + 1) * LANES] + sl.astype(BF16))
            consts.append(-slope2 * t_f)
        q1.append(jnp.concatenate(rows, axis=0))
        cq.append(jnp.concatenate(consts, axis=1))

    def softmax_pv(s, valid, vt, want_weights):
        ps, invs = [], []
        for j in range(NSA_GROUP):
            sj = jnp.where(valid, s[:, j * tq:(j + 1) * tq], NEG_INF)
            m = jnp.max(sj, axis=0, keepdims=True)
            ps.append(jnp.exp2(sj - m))
            l = jnp.sum(ps[-1], axis=0, keepdims=True)
            invs.append(jnp.where(m > 0.5 * NEG_INF, 1.0 / l, 0.0))
        if want_weights:
            p_all = jnp.concatenate([p * inv for p, inv in zip(ps, invs)], axis=1)
            return p_all, jnp.dot(vt, p_all.astype(BF16), preferred_element_type=F32)
        pv = jnp.dot(vt, jnp.concatenate(ps, axis=1).astype(BF16), preferred_element_type=F32)
        return None, pv * jnp.concatenate(invs, axis=1)

    def vrows(g):
        return slice(g * HEAD_DIM, (g + 1) * HEAD_DIM)

    def kcols(g):
        return slice(g * LANES, (g + 1) * LANES)

    r = lax.broadcasted_iota(jnp.int32, (n_cmp, tq), 0)
    blk_id = jnp.where(r < n_sel, 2 * r, 2 * (r - n_sel) + 1)
    valid_c = t_row - (blk_id * CMP_BLOCK + CMP_BLOCK - 1) >= 0
    o_c, score = [], []
    jid = lax.broadcasted_iota(jnp.int32, (n_sel, tq), 0)
    forced = (jid == 0) | (jid == cur)
    for g in groups:
        s_c = lax.dot_general(kc_ref[0, :, kcols(g)], q1[g], NT_DIMS, preferred_element_type=F32)
        p_c, oc = softmax_pv(s_c, valid_c, vct_ref[0, vrows(g), :], True)
        o_c.append(oc)
        imp = p_c[:, 0:tq] + p_c[:, tq:2 * tq] + p_c[:, 2 * tq:3 * tq]
        imp = imp[0:n_sel, :] + imp[n_sel:2 * n_sel, :]
        score.append(jnp.where(jid > cur, -1.0, jnp.where(forced, NSA_GROUP + 1.0, imp)))

    for _ in range(top_k):
        for g in groups:
            mx = jnp.max(score[g], axis=0, keepdims=True)
            first = jnp.min(jnp.where(score[g] == mx, jid, n_sel), axis=0, keepdims=True)
            score[g] = jnp.where(jid == first, -2.0, score[g])
    sel = [jnp.where(score[g] == -2.0, 1.0, 0.0) for g in groups]

    w0 = pl.multiple_of(jnp.maximum(q0 - WINDOW, 0), tq)
    dist_w = t_row - (w0 + lax.broadcasted_iota(jnp.int32, (kw, tq), 0))
    valid_w = (dist_w >= 0) & (dist_w < WINDOW)
    o_w = []
    for g in groups:
        s_w = lax.dot_general(kwin_ref[0, pl.ds(w0, kw), kcols(g)], q1[g], NT_DIMS, preferred_element_type=F32)
        o_w.append(softmax_pv(s_w, valid_w, vwint_ref[0, vrows(g), pl.ds(w0, kw)], False)[1])

    kpos = q0 + lax.broadcasted_iota(jnp.int32, (tq, tq), 0)
    valid_o = tile3((kpos // SEL_BLOCK == cur) & (kpos <= t_row))
    states, q2, counts = [], [], []
    past = jid < cur
    tile_of_blk = (lax.broadcasted_iota(jnp.int32, (max(n_tiles, 8), n_sel), 1) // (kt // SEL_BLOCK)
                   == lax.broadcasted_iota(jnp.int32, (max(n_tiles, 8), n_sel), 0)).astype(BF16)
    for g in groups:
        s = lax.dot_general(kslc_ref[0, pl.ds(q0, tq), kcols(g)], q1[g], NT_DIMS, preferred_element_type=F32)
        s = jnp.where(valid_o, s, -jnp.inf)
        m0 = jnp.max(s, axis=0, keepdims=True) + cq[g]
        p = jnp.exp2(s - (m0 - cq[g]))
        acc0 = jnp.dot(vslct_ref[0, vrows(g), pl.ds(q0, tq)], p.astype(BF16), preferred_element_type=F32)
        s_buf[g] = jnp.full((kt, wide), -jnp.inf, F32)
        p_buf[g] = jnp.zeros((kt, wide), BF16)
        states.append((jnp.full((1, wide), -jnp.inf, F32), jnp.ones((1, wide), F32), m0, sum8(p), acc0))

        sel_sw = jnp.where(past, sel[g], 0.0)
        sel_pad = jnp.concatenate([sel_sw, jnp.zeros((LANES - n_sel, tq), F32)], axis=0) if n_sel < LANES else sel_sw
        mask_lanes = ((sel_pad.T - 1.0) * MASK_BIG).astype(BF16)
        q2.append(jnp.concatenate([q1[g], jnp.concatenate([mask_lanes] * NSA_GROUP, axis=0)], axis=1))
        per_tile = jnp.dot(tile_of_blk, sel_sw.astype(BF16), preferred_element_type=F32)
        counts.append(jnp.dot(per_tile.astype(BF16), jnp.ones((tq, LANES), BF16), preferred_element_type=F32))

    n_act = []
    for g in groups:
        for t in range(n_tiles + 1):
            idx_ref[g, t] = n_tiles
        c = jnp.int32(0)
        for t in range(n_tiles):
            idx_ref[g, c] = t
            c = c + (counts[g][t, 0] > 0.5).astype(jnp.int32)
        idx_ref[g, c] = n_tiles
        n_act.append(c)
    n_trips = jnp.maximum(n_act[0], n_act[1])

    def stage_a(g, tile):
        k0 = pl.multiple_of(tile * kt, kt)
        kk = jnp.concatenate([kslc_ref[0, pl.ds(k0, kt), kcols(g)], e_ref[pl.ds(k0, kt), :]], axis=1)
        s = lax.dot_general(kk, q2[g], NT_DIMS, preferred_element_type=F32)
        s_buf[g] = s
        return jnp.max(s, axis=0, keepdims=True)

    def stage_b(g, tmax, m_i, l_i):
        m_new = jnp.maximum(m_i, tmax + cq[g])
        p = jnp.exp2(s_buf[g] - (m_new - cq[g]))
        alpha = jnp.exp2(m_i - m_new)
        p_buf[g] = p.astype(BF16)
        return alpha, m_new, alpha * l_i + sum8(p)

    def stage_c(g, tile, alpha, acc):
        k0 = pl.multiple_of(tile * kt, kt)
        pv = jnp.dot(vslct_ref[0, vrows(g), pl.ds(k0, kt)], p_buf[g], preferred_element_type=F32)
        return alpha * acc + pv

    def step(g, i, state, run_a):
        tmax, alpha, m_i, l_i, acc = state
        acc = stage_c(g, idx_ref[g, jnp.maximum(i - 2, 0)], alpha, acc)
        alpha, m_i, l_i = stage_b(g, tmax, m_i, l_i)
        if run_a:
            tmax = stage_a(g, idx_ref[g, i])
        return tmax, alpha, m_i, l_i, acc

    states = lax.fori_loop(0, n_trips, lambda i, st: tuple(step(g, i, st[g], True) for g in groups), tuple(states))
    states = tuple(step(g, n_trips, states[g], False) for g in groups)

    out_rows = [None] * NSA_HEADS
    for g in groups:
        _, alpha, _, l_f, acc = states[g]
        acc = stage_c(g, idx_ref[g, jnp.maximum(n_trips - 1, 0)], alpha, acc)
        o_s = acc * (1.0 / jnp.sum(l_f, axis=0, keepdims=True))
        for j in range(NSA_GROUP):
            h = g * NSA_GROUP + j
            gt = jax.nn.sigmoid(gate_ref[0, 3 * h:3 * h + 3, :])
            sl = slice(j * tq, (j + 1) * tq)
            out_rows[h] = gt[0:1] * o_c[g][:, sl] + gt[1:2] * o_s[:, sl] + gt[2:3] * o_w[g][:, sl]
    o_ref[0] = jnp.concatenate(out_rows, axis=0).T.astype(o_ref.dtype)


def _nsa_attention(nq, kc, vct, kslc, vslct, kwin, vwint, gates_t):
    b, s, _ = nq.shape
    n_cmp = s // CMP_BLOCK
    n_sel = s // SEL_BLOCK
    n_tiles = s // NSA_KT
    assert SEL_TOPK < n_sel <= LANES and s % NSA_KT == 0 and s >= WINDOW + NSA_TQ
    pad = NSA_KT
    kslc = jnp.pad(kslc, ((0, 0), (0, pad), (0, 0)))
    vslct = jnp.pad(vslct, ((0, 0), (0, 0), (0, pad)))
    e_sel = jnp.arange(s)[:, None] // SEL_BLOCK == jnp.arange(LANES)[None, :]
    e_sel = jnp.concatenate([e_sel, jnp.ones((pad, LANES), bool)], axis=0).astype(BF16)
    per_b = lambda shape: pl.BlockSpec((1,) + shape, lambda bi, i: (bi, 0, 0))
    wide = NSA_GROUP * NSA_TQ
    return pl.pallas_call(
        functools.partial(_nsa_kernel, seq_len=s),
        grid=(b, s // NSA_TQ),
        in_specs=[pl.BlockSpec((1, NSA_TQ, NSA_Q_COLS), lambda bi, i: (bi, i, 0)),
                  per_b((n_cmp, NSA_K_COLS)), per_b((LANES, n_cmp)),
                  per_b((s + pad, NSA_K_COLS)), per_b((LANES, s + pad)),
                  per_b((s, NSA_K_COLS)), per_b((LANES, s)),
                  pl.BlockSpec((1, gates_t.shape[1], NSA_TQ), lambda bi, i: (bi, 0, i)),
                  pl.BlockSpec(e_sel.shape, lambda bi, i: (0, 0))],
        out_specs=pl.BlockSpec((1, NSA_TQ, NSA_WIDTH), lambda bi, i: (bi, i, 0)),
        out_shape=jax.ShapeDtypeStruct((b, s, NSA_WIDTH), BF16),
        scratch_shapes=[pltpu.VMEM((NSA_KV_HEADS, NSA_KT, wide), F32), pltpu.VMEM((NSA_KV_HEADS, NSA_KT, wide), BF16),
                        pltpu.SMEM((NSA_KV_HEADS, n_tiles + 1), jnp.int32)],
        compiler_params=_cparams("parallel", "arbitrary"),
        name="nsa_attention",
    )(nq, kc, vct, kslc, vslct, kwin, vwint, gates_t, e_sel)


def _cumsum_kernel(f_ref, b_ref, c_ref, piece_ref):
    s = f_ref.shape[-1]
    ri = lax.broadcasted_iota(jnp.int32, (LANES, LANES), 0)
    ci = lax.broadcasted_iota(jnp.int32, (LANES, LANES), 1)
    upper = (ri <= ci).astype(F32)
    ones = jnp.ones((LANES, LANES), F32)

    def chunk(c, carry):
        c0 = pl.multiple_of(c * LANES, LANES)
        lf = jax.nn.log_sigmoid(f_ref[0, :, pl.ds(c0, LANES)] + b_ref[...])
        within = jnp.dot(lf, upper, preferred_element_type=F32, precision=lax.Precision.HIGHEST)
        total = jnp.dot(lf, ones, preferred_element_type=F32, precision=lax.Precision.HIGHEST)
        c = (within + carry) * LOG2E
        c_ref[0, :, pl.ds(c0, LANES)] = c
        rest = -c
        for i in range(BIAS_PIECES):
            piece = rest.astype(BF16).astype(F32)
            piece_ref[0, i, :, pl.ds(c0, LANES)] = piece
            rest = rest - piece
        return carry + total

    lax.fori_loop(0, s // LANES, chunk, jnp.zeros((f_ref.shape[1], LANES), F32))


def _fox_cumsum(f_t, b_f):
    b, r, s = f_t.shape
    bias = jnp.broadcast_to(jnp.pad(b_f, (0, r - b_f.shape[0]))[:, None], (r, LANES)).astype(F32)
    return pl.pallas_call(
        _cumsum_kernel,
        grid=(b,),
        in_specs=[pl.BlockSpec((1, r, s), lambda i: (i, 0, 0)),
                  pl.BlockSpec((r, LANES), lambda i: (0, 0))],
        out_specs=[pl.BlockSpec((1, r, s), lambda i: (i, 0, 0)),
                   pl.BlockSpec((1, BIAS_PIECES, r, s), lambda i: (i, 0, 0, 0))],
        out_shape=[jax.ShapeDtypeStruct((b, r, s), F32),
                   jax.ShapeDtypeStruct((b, BIAS_PIECES, r, s), F32)],
        compiler_params=_cparams("parallel"),
        name="fox_cumsum",
    )(f_t, bias)


FOX_TQ = 512
FOX_KT = 512


def _fox_kernel(q_ref, k_ref, kb_ref, vt_ref, cq_ref, o_ref, s_buf, p_buf):
    tq, kt = FOX_TQ, FOX_KT
    n_full = pl.program_id(2)
    q0 = n_full * tq
    lane = lax.broadcasted_iota(jnp.int32, (tq, LANES), 1)
    t_row = q0 + lax.broadcasted_iota(jnp.int32, (1, tq), 1)
    qh = []
    for hh in range(2):
        pick = (lane >= hh * BIAS_PIECES) & (lane < (hh + 1) * BIAS_PIECES)
        qh.append(jnp.concatenate([jnp.where(lane // HEAD_DIM == hh, q_ref[0], 0).astype(BF16),
                                   jnp.where(pick, 1.0, 0.0).astype(BF16)], axis=1))
    cq = [cq_ref[0, hh] for hh in range(2)]

    def stage_a(hh, j, masked):
        k0 = pl.multiple_of(j * kt, kt)
        kk = jnp.concatenate([k_ref[0, pl.ds(k0, kt), :], kb_ref[0, 0, pl.ds(k0, kt), :]], axis=1)
        s = lax.dot_general(kk, qh[hh], NT_DIMS, preferred_element_type=F32)
        if masked:
            s = jnp.where(t_row >= k0 + lax.broadcasted_iota(jnp.int32, (kt, tq), 0), s, -jnp.inf)
        s_buf[hh] = s
        return jnp.max(s, axis=0, keepdims=True)

    def stage_b(hh, tmax, m_i, l_i):
        m_new = jnp.maximum(m_i, tmax + cq[hh])
        p = jnp.exp2(s_buf[hh] - (m_new - cq[hh]))
        alpha = jnp.exp2(m_i - m_new)
        l_new = alpha * l_i + jnp.sum(p.reshape(kt // 8, 8, tq), axis=0)
        p_buf[hh] = p.astype(BF16)
        return alpha, m_new, l_new

    def stage_c(hh, j, alpha, acc):
        k0 = pl.multiple_of(jnp.maximum(j, 0) * kt, kt)
        pv = jnp.dot(vt_ref[0, hh * HEAD_DIM:(hh + 1) * HEAD_DIM, pl.ds(k0, kt)], p_buf[hh],
                     preferred_element_type=F32)
        return alpha * acc + pv

    def init_state(hh):
        s_buf[hh] = jnp.full((kt, tq), -jnp.inf, F32)
        p_buf[hh] = jnp.zeros((kt, tq), BF16)
        return (jnp.full((1, tq), -jnp.inf, F32), jnp.ones((1, tq), F32), jnp.full((1, tq), NEG_INF, F32),
                jnp.zeros((8, tq), F32), jnp.zeros((HEAD_DIM, tq), F32))

    def step(hh, i, state, run_a, masked=False):
        tmax, alpha, m_i, l_i, acc = state
        acc = stage_c(hh, i - 2, alpha, acc)
        alpha, m_i, l_i = stage_b(hh, tmax, m_i, l_i)
        if run_a:
            tmax = stage_a(hh, i, masked)
        return tmax, alpha, m_i, l_i, acc

    def trip(i, states):
        return tuple(step(hh, i, states[hh], True) for hh in range(2))

    states = lax.fori_loop(0, n_full, trip, (init_state(0), init_state(1)))
    states = tuple(step(hh, n_full, states[hh], True, masked=True) for hh in range(2))
    states = tuple(step(hh, n_full + 1, states[hh], False) for hh in range(2))
    outs = []
    for hh in range(2):
        _, alpha, _, l_f, acc = states[hh]
        acc = stage_c(hh, n_full, alpha, acc)
        outs.append(acc / jnp.sum(l_f, axis=0, keepdims=True))
    o_ref[0] = jnp.concatenate(outs, axis=0).T.astype(o_ref.dtype)


def _fox_attention(fq, fk, kb, fvt, c_rows):
    b, s, _ = fq.shape
    n_pair = FOX_HEADS // 2
    return pl.pallas_call(
        _fox_kernel,
        grid=(b, n_pair, s // FOX_TQ),
        in_specs=[pl.BlockSpec((1, FOX_TQ, LANES), lambda bi, p, i: (bi, i, p)),
                  pl.BlockSpec((1, s, LANES), lambda bi, p, i: (bi, 0, p)),
                  pl.BlockSpec((1, 1, s, LANES), lambda bi, p, i: (bi, p, 0, 0)),
                  pl.BlockSpec((1, LANES, s), lambda bi, p, i: (bi, p, 0)),
                  pl.BlockSpec((1, 2, 1, FOX_TQ), lambda bi, p, i: (bi, p, 0, i))],
        out_specs=pl.BlockSpec((1, FOX_TQ, LANES), lambda bi, p, i: (bi, i, p)),
        out_shape=jax.ShapeDtypeStruct((b, s, FOX_WIDTH), BF16),
        scratch_shapes=[pltpu.VMEM((2, FOX_KT, FOX_TQ), F32), pltpu.VMEM((2, FOX_KT, FOX_TQ), BF16)],
        compiler_params=_cparams("parallel", "parallel", "arbitrary"),
        name="fox_attention",
    )(fq, fk, kb, fvt, c_rows)


CONV_TS = 512
CONV_HALO = 32


def _conv_kernel(u_ref, uh_ref, w_ref, b_ref, g_ref, be_ref, o_ref, y_ref, *, ksize):
    ts, halo, cw = CONV_TS, CONV_HALO, CONV_WIDTH

    def glu(u):
        return u[:, :cw] * jax.nn.sigmoid(u[:, cw:])

    first = pl.program_id(1) == 0
    y_ref[0:halo, :] = jnp.where(first, 0.0, glu(uh_ref[0]))
    y_ref[halo:halo + ts, :] = glu(u_ref[0])
    acc = jnp.zeros((ts, cw), F32)
    for k in range(ksize):
        start = halo - (ksize - 1) + k
        acc = acc + w_ref[k:k + 1, :] * y_ref[start:start + ts, :]
    hn = _ln_rows(acc + b_ref[...], g_ref[...], be_ref[...])
    o_ref[0] = jax.nn.silu(hn).astype(o_ref.dtype)


def _conv_mixer(u, w_dw, b_dw, ln_g, ln_b):
    b, s, _ = u.shape
    ksize = w_dw.shape[0]
    assert ksize - 1 <= CONV_HALO
    ratio = CONV_TS // CONV_HALO
    vec = lambda a: a.reshape(1, CONV_WIDTH)
    const = lambda shape: pl.BlockSpec(shape, lambda bi, i: (0, 0))
    return pl.pallas_call(
        functools.partial(_conv_kernel, ksize=ksize),
        grid=(b, s // CONV_TS),
        in_specs=[pl.BlockSpec((1, CONV_TS, 2 * CONV_WIDTH), lambda bi, i: (bi, i, 0)),
                  pl.BlockSpec((1, CONV_HALO, 2 * CONV_WIDTH), lambda bi, i: (bi, jnp.maximum(i * ratio - 1, 0), 0)),
                  const((ksize, CONV_WIDTH)), const((1, CONV_WIDTH)), const((1, CONV_WIDTH)),
                  const((1, CONV_WIDTH))],
        out_specs=pl.BlockSpec((1, CONV_TS, CONV_WIDTH), lambda bi, i: (bi, i, 0)),
        out_shape=jax.ShapeDtypeStruct((b, s, CONV_WIDTH), BF16),
        scratch_shapes=[pltpu.VMEM((CONV_HALO + CONV_TS, CONV_WIDTH), F32)],
        compiler_params=_cparams("parallel", "arbitrary"),
        name="conv_mixer",
    )(u, u, w_dw, vec(b_dw), vec(ln_g), vec(ln_b))


def _outproj_kernel(oa_ref, ob_ref, oc_ref, x_ref, wa_ref, wb_ref, wc_ref, g_ref, b_ref, o_ref, *, alpha):
    mix = jnp.dot(oa_ref[...], wa_ref[...], preferred_element_type=F32)
    mix = mix + jnp.dot(ob_ref[...], wb_ref[...], preferred_element_type=F32)
    mix = mix + jnp.dot(oc_ref[...], wc_ref[...], preferred_element_type=F32)
    o_ref[...] = _ln_rows(alpha * x_ref[...] + mix, g_ref[...], b_ref[...])


def _out_proj(o_a, o_b, o_c, x2d, w_out, g, b, alpha, tm=512):
    t, d = x2d.shape
    wa = w_out[:NSA_WIDTH].astype(BF16)
    wb = w_out[NSA_WIDTH:NSA_WIDTH + CONV_WIDTH].astype(BF16)
    wc = w_out[NSA_WIDTH + CONV_WIDTH:].astype(BF16)
    rows = lambda w: pl.BlockSpec((tm, w), lambda i: (i, 0))
    full = lambda a: pl.BlockSpec(a.shape, lambda i: (0, 0))
    return pl.pallas_call(
        functools.partial(_outproj_kernel, alpha=alpha),
        grid=(t // tm,),
        in_specs=[rows(NSA_WIDTH), rows(CONV_WIDTH), rows(FOX_WIDTH), rows(d), full(wa), full(wb), full(wc),
                  pl.BlockSpec((1, d), lambda i: (0, 0)), pl.BlockSpec((1, d), lambda i: (0, 0))],
        out_specs=rows(d),
        out_shape=jax.ShapeDtypeStruct((t, d), F32),
        compiler_params=_cparams("parallel"),
        name="out_proj_ln",
    )(o_a, o_b, o_c, x2d, wa, wb, wc, g.reshape(1, d), b.reshape(1, d))


FFN_TM = 512
FFN_HALO = 8
FFN_FC = 256


def _ffn_kernel(x_ref, xh_ref, wg_ref, wu_ref, wd_ref, cw_ref, cb_ref, g_ref, b_ref, o_ref, gate_ref,
                *, alpha, n_chunks, ksize):
    tm, halo, fc = FFN_TM, FFN_HALO, FFN_FC
    x = x_ref[0]
    xb = x.astype(BF16)
    xhb = xh_ref[0].astype(BF16)
    first = pl.program_id(1) == 0
    acc = jnp.zeros((tm, x.shape[-1]), F32)
    for c in range(n_chunks):
        cs = slice(c * fc, (c + 1) * fc)
        gate_ref[0:halo, :] = jnp.where(first, 0.0, jnp.dot(xhb, wg_ref[:, cs], preferred_element_type=F32))
        gate_ref[halo:halo + tm, :] = jnp.dot(xb, wg_ref[:, cs], preferred_element_type=F32)
        conv = cb_ref[:, cs]
        for k in range(ksize):
            start = halo - (ksize - 1) + k
            conv = conv + cw_ref[k:k + 1, cs] * gate_ref[start:start + tm, :]
        up = jnp.dot(xb, wu_ref[:, cs], preferred_element_type=F32)
        h = (jax.nn.silu(conv) * up).astype(BF16)
        acc = acc + jnp.dot(h, wd_ref[cs, :], preferred_element_type=F32)
    o_ref[0] = _ln_rows(alpha * x + acc, g_ref[...], b_ref[...])


def _conv_ffn(x3d, w_ffn_in, w_conv, b_conv, w_down, g, b, alpha):
    bsz, s, d = x3d.shape
    d_ff = w_down.shape[0]
    ksize = w_conv.shape[0]
    assert d_ff % FFN_FC == 0 and ksize - 1 <= FFN_HALO
    wg = w_ffn_in[:, :d_ff].astype(BF16)
    wu = w_ffn_in[:, d_ff:].astype(BF16)
    wd = w_down.astype(BF16)
    ratio = FFN_TM // FFN_HALO
    const = lambda shape: pl.BlockSpec(shape, lambda bi, i: (0, 0))
    return pl.pallas_call(
        functools.partial(_ffn_kernel, alpha=alpha, n_chunks=d_ff // FFN_FC, ksize=ksize),
        grid=(bsz, s // FFN_TM),
        in_specs=[pl.BlockSpec((1, FFN_TM, d), lambda bi, i: (bi, i, 0)),
                  pl.BlockSpec((1, FFN_HALO, d), lambda bi, i: (bi, jnp.maximum(i * ratio - 1, 0), 0)),
                  const((d, d_ff)), const((d, d_ff)), const((d_ff, d)),
                  const((ksize, d_ff)), const((1, d_ff)), const((1, d)), const((1, d))],
        out_specs=pl.BlockSpec((1, FFN_TM, d), lambda bi, i: (bi, i, 0)),
        out_shape=jax.ShapeDtypeStruct((bsz, s, d), F32),
        scratch_shapes=[pltpu.VMEM((FFN_HALO + FFN_TM, FFN_FC), F32)],
        compiler_params=_cparams("parallel", "arbitrary"),
        name="conv_ffn_ln",
    )(x3d, x3d, wg, wu, wd, w_conv, b_conv.reshape(1, d_ff), g.reshape(1, d), b.reshape(1, d))


def _cmp_blocks(raw, bsz, s):
    n_cmp = s // CMP_BLOCK
    t = raw.reshape(bsz, n_cmp, CMP_BLOCK, NSA_KV_HEADS, HEAD_DIM).transpose(0, 3, 1, 2, 4)
    return t.reshape(bsz * NSA_KV_HEADS * n_cmp, CMP_BLOCK * HEAD_DIM)


def _cmp_even_odd(c, bsz, s):
    n_cmp = s // CMP_BLOCK
    c = c.reshape(bsz, NSA_KV_HEADS, n_cmp // 2, 2, HEAD_DIM).transpose(0, 3, 2, 1, 4)
    return c.reshape(bsz, n_cmp, NSA_KV_HEADS, HEAD_DIM)


def kernel(x, ln_emb_g, ln_emb_b, w_in, b_f, w_cmp1, w_cmp2, pe_cmp, w_dw, b_dw, ln_conv_g, ln_conv_b, w_out,
           ln1_g, ln1_b, w_ffn_in, w_ffn_conv, b_ffn_conv, w_ffn_down, ln2_g, ln2_b):
    bsz, s, d = x.shape
    depth = w_in.shape[0]
    alpha = (2.0 * depth) ** 0.25
    t = bsz * s
    n_cmp = s // CMP_BLOCK
    blk = jnp.concatenate([jnp.arange(0, n_cmp, 2), jnp.arange(1, n_cmp, 2)])
    cmp_end_lanes = _position_lanes(blk * CMP_BLOCK + CMP_BLOCK - 1)[:, BIAS_LANE0:]

    xc = _layer_norm(x.reshape(t, d), ln_emb_g, ln_emb_b)
    for l in range(depth):
        (nq, kslc, kwin, fq, fk, cmp_raw, conv_u, vslct, vwint, fvt, gates_t, f_t) = _project(
            xc.reshape(bsz, s, d), *_split_w_in(w_in[l]))

        kvw = NSA_KV_HEADS * HEAD_DIM
        xblk = jnp.stack([_cmp_blocks(cmp_raw[..., :kvw], bsz, s), _cmp_blocks(cmp_raw[..., kvw:], bsz, s)])
        cmp_out = _compress(xblk, pe_cmp[l].reshape(2, 1, CMP_BLOCK * HEAD_DIM), w_cmp1[l], w_cmp2[l])
        kc = _cmp_even_odd(cmp_out[0], bsz, s)
        kc = jnp.concatenate([kc, jnp.broadcast_to(cmp_end_lanes[None, :, None, :], kc.shape)], axis=-1)
        kc = kc.reshape(bsz, n_cmp, NSA_K_COLS).astype(BF16)
        vct = _cmp_even_odd(cmp_out[1], bsz, s).reshape(bsz, n_cmp, kvw).astype(BF16).transpose(0, 2, 1)
        o_a = _nsa_attention(nq, kc, vct, kslc, vslct, kwin, vwint, gates_t)

        o_b = _conv_mixer(conv_u, w_dw[l], b_dw[l], ln_conv_g[l], ln_conv_b[l])

        c, pieces = _fox_cumsum(f_t, b_f[l])
        c_rows = c[:, :FOX_HEADS].reshape(bsz, FOX_HEADS, 1, s)
        kb = pieces[:, :, :FOX_HEADS].reshape(bsz, BIAS_PIECES, FOX_HEADS // 2, 2, s)
        kb = kb.transpose(0, 2, 4, 3, 1).reshape(bsz, FOX_HEADS // 2, s, 2 * BIAS_PIECES)
        kb = jnp.pad(kb, ((0, 0), (0, 0), (0, 0), (0, LANES - 2 * BIAS_PIECES))).astype(BF16)
        o_c = _fox_attention(fq, fk, kb, fvt, c_rows)

        xc = _out_proj(o_a.reshape(t, NSA_WIDTH), o_b.reshape(t, CONV_WIDTH), o_c.reshape(t, FOX_WIDTH),
                       xc, w_out[l], ln1_g[l], ln1_b[l], alpha)
        xc = _conv_ffn(xc.reshape(bsz, s, d), w_ffn_in[l], w_ffn_conv[l], b_ffn_conv[l], w_ffn_down[l],
                       ln2_g[l], ln2_b[l], alpha).reshape(t, d)
    return xc.reshape(bsz, s, d)
```

```python
import functools

import jax
import jax.numpy as jnp
import numpy as np
from jax import lax
from jax.experimental import pallas as pl
from jax.experimental.pallas import tpu as pltpu

HEAD_DIM = 64
NSA_HEADS = 6
NSA_KV_HEADS = 2
NSA_GROUP = 3
FOX_HEADS = 6
NSA_WIDTH = NSA_HEADS * HEAD_DIM
CONV_WIDTH = 256
FOX_WIDTH = FOX_HEADS * HEAD_DIM
CMP_BLOCK = 32
SEL_BLOCK = 64
SEL_TOPK = 16
WINDOW = 512
LN_EPS = 1e-5
NEG_INF = -1e30
SCALE = HEAD_DIM ** -0.5
LOG2E = 1.4426950408889634
BIAS_PIECES = 3
BIAS_LANE0 = HEAD_DIM
PAD_LANE = BIAS_LANE0 + 2 * BIAS_PIECES
VROWS = HEAD_DIM + 16
MASK_BIG = 2.0 ** 100

LANES = 128
VMEM_LIMIT = 56 * 1024 * 1024

BF16 = jnp.bfloat16
F32 = jnp.float32
NT_DIMS = (((1,), (1,)), ((), ()))


def _cparams(*sem):
    return pltpu.CompilerParams(dimension_semantics=sem, vmem_limit_bytes=VMEM_LIMIT)


def _ln_rows(x, g, b):
    mu = jnp.mean(x, axis=-1, keepdims=True)
    xc = x - mu
    var = jnp.mean(xc * xc, axis=-1, keepdims=True)
    return xc * lax.rsqrt(var + LN_EPS) * g + b


def _bf16_pieces(value, n=BIAS_PIECES):
    out = []
    rest = np.float32(value)
    for _ in range(n):
        bits = np.float32(rest).view(np.uint32)
        bits = (bits + np.uint32(0x7FFF) + ((bits >> np.uint32(16)) & np.uint32(1))) & np.uint32(0xFFFF0000)
        piece = bits.view(np.float32)
        out.append(float(piece))
        rest = np.float32(rest - piece)
    return out


def _position_lanes(pos):
    hi = ((pos // SEL_BLOCK) * SEL_BLOCK).astype(F32)
    lo = (pos % SEL_BLOCK).astype(F32)
    lanes = jnp.zeros((pos.shape[0], LANES), F32)
    for i in range(BIAS_PIECES):
        lanes = lanes.at[:, BIAS_LANE0 + 2 * i].set(hi)
        lanes = lanes.at[:, BIAS_LANE0 + 2 * i + 1].set(lo)
    return lanes


def _ln_kernel(x_ref, g_ref, b_ref, o_ref):
    o_ref[...] = _ln_rows(x_ref[...], g_ref[...], b_ref[...])


def _layer_norm(x2d, g, b, tm=1024):
    t, d = x2d.shape
    return pl.pallas_call(
        _ln_kernel,
        grid=(t // tm,),
        in_specs=[pl.BlockSpec((tm, d), lambda i: (i, 0)),
                  pl.BlockSpec((1, d), lambda i: (0, 0)),
                  pl.BlockSpec((1, d), lambda i: (0, 0))],
        out_specs=pl.BlockSpec((tm, d), lambda i: (i, 0)),
        out_shape=jax.ShapeDtypeStruct((t, d), F32),
        compiler_params=_cparams("parallel"),
        name="ln_emb",
    )(x2d, g.reshape(1, d), b.reshape(1, d))


NSA_Q_COLS = NSA_HEADS * LANES
NSA_K_COLS = NSA_KV_HEADS * LANES
ROWS_SPLIT = (NSA_Q_COLS, NSA_K_COLS, NSA_K_COLS, FOX_WIDTH, FOX_WIDTH)
F32_SPLIT = (128, 128, 2 * CONV_WIDTH)
T_SPLIT = (128, 128, FOX_WIDTH, 24, 8)
T_ROWS = tuple(w // HEAD_DIM * VROWS for w in T_SPLIT[:3]) + T_SPLIT[3:]


def _proj_kernel(x_ref, wr_ref, wf_ref, wt_ref, pos_ref,
                 nq_ref, kslc_ref, kwin_ref, fq_ref, fk_ref, cmpk_ref, cmpv_ref, conv_ref,
                 vslc_ref, vwin_ref, fv_ref, gate_ref, flog_ref):
    xb = x_ref[0].astype(BF16)
    rows = jnp.dot(xb, wr_ref[...], preferred_element_type=F32)
    off = 0
    for ref, w in zip((nq_ref, kslc_ref, kwin_ref, fq_ref, fk_ref), ROWS_SPLIT):
        blk = rows[:, off:off + w]
        if ref is kslc_ref or ref is kwin_ref:
            blk = blk + pos_ref[...]
        ref[0] = blk.astype(ref.dtype)
        off += w
    f32o = jnp.dot(xb, wf_ref[...], preferred_element_type=F32)
    off = 0
    for ref, w in zip((cmpk_ref, cmpv_ref, conv_ref), F32_SPLIT):
        ref[0] = f32o[:, off:off + w]
        off += w
    tr = lax.dot_general(wt_ref[...], xb, NT_DIMS, preferred_element_type=F32)
    ones = jnp.ones((VROWS - HEAD_DIM, tr.shape[1]), F32)
    off = 0
    for ref, w in zip((vslc_ref, vwin_ref, fv_ref, gate_ref, flog_ref), T_SPLIT):
        blk = tr[off:off + w, :]
        if ref.shape[1] != w:
            blk = jnp.concatenate([part for h in range(w // HEAD_DIM)
                                   for part in (blk[h * HEAD_DIM:(h + 1) * HEAD_DIM], ones)], axis=0)
        ref[0] = blk.astype(ref.dtype)
        off += w


def _split_w_in(w_in_l):
    kvw = NSA_KV_HEADS * HEAD_DIM
    o = 0
    w_nq = w_in_l[:, o:o + NSA_WIDTH]; o += NSA_WIDTH
    w_kv = w_in_l[:, o:o + 6 * kvw]; o += 6 * kvw
    w_g = w_in_l[:, o:o + 3 * NSA_HEADS]; o += 3 * NSA_HEADS
    w_conv = w_in_l[:, o:o + 2 * CONV_WIDTH]; o += 2 * CONV_WIDTH
    w_fqkv = w_in_l[:, o:o + 3 * FOX_WIDTH]; o += 3 * FOX_WIDTH
    w_f = w_in_l[:, o:o + FOX_HEADS]
    d = w_in_l.shape[0]

    def head_blocks(w, n_heads):
        w = w.reshape(d, n_heads, HEAD_DIM)
        return jnp.pad(w, ((0, 0), (0, 0), (0, LANES - HEAD_DIM))).reshape(d, n_heads * LANES)

    w_nq = head_blocks(w_nq * (SCALE * LOG2E), NSA_HEADS)
    kv = [w_kv[:, i * kvw:(i + 1) * kvw] for i in range(6)]
    w_fq = w_fqkv[:, :FOX_WIDTH] * (SCALE * LOG2E)
    w_fk = w_fqkv[:, FOX_WIDTH:2 * FOX_WIDTH]
    w_fv = w_fqkv[:, 2 * FOX_WIDTH:]
    w_rows = jnp.concatenate([w_nq, head_blocks(kv[2], NSA_KV_HEADS), head_blocks(kv[4], NSA_KV_HEADS),
                              w_fq, w_fk], axis=1).astype(BF16)
    w_f32 = jnp.concatenate([kv[0], kv[1], w_conv], axis=1).astype(BF16)
    w_gp = jnp.pad(w_g, ((0, 0), (0, T_SPLIT[3] - w_g.shape[1])))
    w_fp = jnp.pad(w_f, ((0, 0), (0, T_SPLIT[4] - w_f.shape[1])))
    w_t = jnp.concatenate([kv[3], kv[5], w_fv, w_gp, w_fp], axis=1).T.astype(BF16)
    return w_rows, w_f32, w_t


def _project(x3d, w_rows, w_f32, w_t, tm=512):
    b, s, d = x3d.shape
    pos_lanes = jnp.tile(_position_lanes(jnp.arange(s)), (1, NSA_KV_HEADS))
    row_spec = lambda w: pl.BlockSpec((1, tm, w), lambda bi, i: (bi, i, 0))
    t_spec = lambda w: pl.BlockSpec((1, w, tm), lambda bi, i: (bi, 0, i))
    full = lambda a: pl.BlockSpec(a.shape, lambda bi, i: (0, 0))
    out_shape = ([jax.ShapeDtypeStruct((b, s, w), BF16) for w in ROWS_SPLIT]
                 + [jax.ShapeDtypeStruct((b, s, w), F32) for w in F32_SPLIT]
                 + [jax.ShapeDtypeStruct((b, w, s), dt) for w, dt in zip(T_ROWS, (BF16, BF16, BF16, F32, F32))])
    out_specs = [row_spec(w) for w in ROWS_SPLIT + F32_SPLIT] + [t_spec(w) for w in T_ROWS]
    return pl.pallas_call(
        _proj_kernel,
        grid=(b, s // tm),
        in_specs=[row_spec(d), full(w_rows), full(w_f32), full(w_t),
                  pl.BlockSpec((tm, NSA_K_COLS), lambda bi, i: (i, 0))],
        out_specs=out_specs,
        out_shape=out_shape,
        compiler_params=_cparams("parallel", "parallel"),
        name="in_proj",
    )(x3d, w_rows, w_f32, w_t, pos_lanes)


def _compress_kernel(rawk_ref, rawv_ref, pe_ref, w1_ref, w2k_ref, w2vt_ref, end_ref, kc_ref, vct_ref):
    s = rawk_ref.shape[1]
    half = s // (2 * CMP_BLOCK)
    hidden = w1_ref.shape[-1] // NSA_KV_HEADS
    outs = []
    for i, raw_ref in enumerate((rawk_ref, rawv_ref)):
        hid = jnp.zeros((2 * half, w1_ref.shape[-1]), F32)
        for p in range(CMP_BLOCK):
            rows = jnp.concatenate(
                [raw_ref[0, pl.ds(par * CMP_BLOCK + p, half, stride=2 * CMP_BLOCK), :]
                 for par in range(2)], axis=0)
            hid = hid + jnp.dot((rows + pe_ref[i, p:p + 1, :]).astype(BF16), w1_ref[i, p],
                                preferred_element_type=F32)
        outs.append(jax.nn.gelu(hid).astype(BF16))
    kc_ref[0] = jnp.concatenate(
        [jnp.dot(outs[0][:, g * hidden:(g + 1) * hidden], w2k_ref[...], preferred_element_type=F32) + end_ref[...]
         for g in range(NSA_KV_HEADS)], axis=1).astype(kc_ref.dtype)
    vct_ref[0] = jnp.concatenate(
        [lax.dot_general(w2vt_ref[...], outs[1][:, g * hidden:(g + 1) * hidden], NT_DIMS,
                         preferred_element_type=F32) for g in range(NSA_KV_HEADS)], axis=0).astype(vct_ref.dtype)


def _compress(cmp_k, cmp_v, pe, w1, w2):
    b, s, _ = cmp_k.shape
    n_cmp = s // CMP_BLOCK
    hidden = w1.shape[-1]
    pe2 = jnp.tile(pe, (1, 1, NSA_KV_HEADS))
    w1p = w1.reshape(2, CMP_BLOCK, HEAD_DIM, hidden)
    zero = jnp.zeros_like(w1p)
    w1blk = jnp.concatenate([jnp.concatenate([w1p, zero], axis=-1),
                             jnp.concatenate([zero, w1p], axis=-1)], axis=2).astype(BF16)
    w2k = jnp.pad(w2[0], ((0, 0), (0, LANES - HEAD_DIM))).astype(BF16)
    w2vt = w2[1].T.astype(BF16)
    blk = jnp.concatenate([jnp.arange(0, n_cmp, 2), jnp.arange(1, n_cmp, 2)])
    end_lanes = _position_lanes(blk * CMP_BLOCK + CMP_BLOCK - 1)
    const = lambda a: pl.BlockSpec(a.shape, lambda i: (0,) * a.ndim)
    return pl.pallas_call(
        _compress_kernel,
        grid=(b,),
        in_specs=[pl.BlockSpec((1, s, LANES), lambda i: (i, 0, 0)), pl.BlockSpec((1, s, LANES), lambda i: (i, 0, 0)),
                  const(pe2), const(w1blk), const(w2k), const(w2vt), const(end_lanes)],
        out_specs=[pl.BlockSpec((1, n_cmp, NSA_K_COLS), lambda i: (i, 0, 0)),
                   pl.BlockSpec((1, NSA_KV_HEADS * HEAD_DIM, n_cmp), lambda i: (i, 0, 0))],
        out_shape=[jax.ShapeDtypeStruct((b, n_cmp, NSA_K_COLS), BF16),
                   jax.ShapeDtypeStruct((b, NSA_KV_HEADS * HEAD_DIM, n_cmp), BF16)],
        compiler_params=_cparams("parallel"),
        name="nsa_compress",
    )(cmp_k, cmp_v, pe2, w1blk, w2k, w2vt, end_lanes)


NSA_TQ = 256
NSA_KT = 512


def _alibi_slope(h):
    return float(2.0 ** (-8.0 * (h + 1) / NSA_HEADS))


def _nsa_kernel(q_ref, kc_ref, vct_ref, kslc_ref, vslct_ref, kwin_ref, vwint_ref, gate_ref, e_ref, o_ref,
                s_buf, p_buf, idx_ref, *, seq_len):
    tq, kt = NSA_TQ, NSA_KT
    n_cmp = seq_len // CMP_BLOCK
    n_sel = seq_len // SEL_BLOCK
    n_tiles = seq_len // kt
    top_k = min(SEL_TOPK, n_sel)
    kw = WINDOW + tq
    wide = NSA_GROUP * tq
    qi = pl.program_id(1)
    q0 = qi * tq
    t_row = q0 + lax.broadcasted_iota(jnp.int32, (1, tq), 1)
    t_f = t_row.astype(F32)
    lane = lax.broadcasted_iota(jnp.int32, (1, LANES), 1)
    cur = t_row // SEL_BLOCK
    groups = range(NSA_KV_HEADS)

    def tile3(a):
        return jnp.concatenate([a] * NSA_GROUP, axis=1)

    q1, cq = [], []
    for g in groups:
        rows, consts = [], []
        for j in range(NSA_GROUP):
            h = g * NSA_GROUP + j
            slope2 = _alibi_slope(h) * LOG2E
            sl = jnp.where(lane == PAD_LANE, -MASK_BIG, jnp.zeros((1, LANES), F32))
            for i, piece in enumerate(_bf16_pieces(slope2)):
                sl = jnp.where((lane == BIAS_LANE0 + 2 * i) | (lane == BIAS_LANE0 + 2 * i + 1), piece, sl)
            rows.append(q_ref[0, :, h * LANES:(h + 1) * LANES] + sl.astype(BF16))
            consts.append(-slope2 * t_f)
        q1.append(jnp.concatenate(rows, axis=0))
        cq.append(jnp.concatenate(consts, axis=1))

    def softmax_weights(s, valid):
        ps = []
        for j in range(NSA_GROUP):
            sj = jnp.where(valid, s[:, j * tq:(j + 1) * tq], NEG_INF)
            m = jnp.max(sj, axis=0, keepdims=True)
            p = jnp.exp2(sj - m)
            l = jnp.sum(p, axis=0, keepdims=True)
            ps.append(p * jnp.where(m > 0.5 * NEG_INF, 1.0 / l, 0.0))
        return jnp.concatenate(ps, axis=1)

    def softmax_out(s, vt):
        m = jnp.max(s, axis=0, keepdims=True)
        pv = jnp.dot(vt, jnp.exp2(s - m).astype(BF16), preferred_element_type=F32)
        return pv[:HEAD_DIM] * (1.0 / pv[HEAD_DIM:HEAD_DIM + 1])

    def vrows(g):
        return slice(g * VROWS, (g + 1) * VROWS)

    def kcols(g):
        return slice(g * LANES, (g + 1) * LANES)

    r = lax.broadcasted_iota(jnp.int32, (n_cmp, tq), 0)
    blk_id = jnp.where(r < n_sel, 2 * r, 2 * (r - n_sel) + 1)
    valid_c = t_row - (blk_id * CMP_BLOCK + CMP_BLOCK - 1) >= 0
    o_c, score = [], []
    jid = lax.broadcasted_iota(jnp.int32, (n_sel, tq), 0)
    forced = (jid == 0) | (jid == cur)
    for g in groups:
        s_c = lax.dot_general(kc_ref[0, :, kcols(g)], q1[g], NT_DIMS, preferred_element_type=F32)
        p_c = softmax_weights(s_c, valid_c)
        o_c.append(jnp.dot(vct_ref[0, g * HEAD_DIM:(g + 1) * HEAD_DIM, :], p_c.astype(BF16),
                           preferred_element_type=F32))
        imp = p_c[:, 0:tq] + p_c[:, tq:2 * tq] + p_c[:, 2 * tq:3 * tq]
        imp = imp[0:n_sel, :] + imp[n_sel:2 * n_sel, :]
        score.append(jnp.where(jid > cur, -1.0, jnp.where(forced, NSA_GROUP + 1.0, imp)))

    for _ in range(top_k):
        for g in groups:
            mx = jnp.max(score[g], axis=0, keepdims=True)
            first = jnp.min(jnp.where(score[g] == mx, jid, n_sel), axis=0, keepdims=True)
            score[g] = jnp.where(jid == first, -2.0, score[g])
    sel = [jnp.where(score[g] == -2.0, 1.0, 0.0) for g in groups]

    w0 = pl.multiple_of(q0, tq)
    row = lax.broadcasted_iota(jnp.int32, (tq, tq), 0)
    col = lax.broadcasted_iota(jnp.int32, (tq, tq), 1)
    near = tile3(col < row)
    causal = tile3(row <= col)
    o_w = []
    for g in groups:
        s_w = lax.dot_general(kwin_ref[0, pl.ds(w0, kw), kcols(g)], q1[g], NT_DIMS, preferred_element_type=F32)
        s_w = jnp.concatenate([jnp.where(near, s_w[:tq], -jnp.inf), s_w[tq:WINDOW],
                               jnp.where(causal, s_w[WINDOW:], -jnp.inf)], axis=0)
        o_w.append(softmax_out(s_w, vwint_ref[0, vrows(g), pl.ds(w0, kw)]))

    kpos = q0 + lax.broadcasted_iota(jnp.int32, (tq, tq), 0)
    valid_o = tile3((kpos // SEL_BLOCK == cur) & (kpos <= t_row))
    states, q2, counts = [], [], []
    past = jid < cur
    tile_of_blk = (lax.broadcasted_iota(jnp.int32, (max(n_tiles, 8), n_sel), 1) // (kt // SEL_BLOCK)
                   == lax.broadcasted_iota(jnp.int32, (max(n_tiles, 8), n_sel), 0)).astype(BF16)
    for g in groups:
        s = lax.dot_general(kslc_ref[0, pl.ds(q0, tq), kcols(g)], q1[g], NT_DIMS, preferred_element_type=F32)
        s = jnp.where(valid_o, s, -jnp.inf)
        m0 = jnp.max(s, axis=0, keepdims=True) + cq[g]
        p = jnp.exp2(s - (m0 - cq[g]))
        acc0 = jnp.dot(vslct_ref[0, vrows(g), pl.ds(q0, tq)], p.astype(BF16), preferred_element_type=F32)
        s_buf[g] = jnp.full((kt, wide), -jnp.inf, F32)
        p_buf[g] = jnp.zeros((kt, wide), BF16)
        states.append((jnp.full((1, wide), -jnp.inf, F32), jnp.ones((1, wide), F32), m0, acc0))

        sel_sw = jnp.where(past, sel[g], 0.0)
        sel_pad = jnp.concatenate([sel_sw, jnp.zeros((LANES - n_sel, tq), F32)], axis=0) if n_sel < LANES else sel_sw
        mask_lanes = ((sel_pad.T - 1.0) * MASK_BIG).astype(BF16)
        q2.append(jnp.concatenate([q1[g], jnp.concatenate([mask_lanes] * NSA_GROUP, axis=0)], axis=1))
        per_tile = jnp.dot(tile_of_blk, sel_sw.astype(BF16), preferred_element_type=F32)
        counts.append(jnp.dot(per_tile.astype(BF16), jnp.ones((tq, LANES), BF16), preferred_element_type=F32))

    n_act = []
    for g in groups:
        for t in range(n_tiles + 1):
            idx_ref[g, t] = n_tiles
        c = jnp.int32(0)
        for t in range(n_tiles):
            idx_ref[g, c] = t
            c = c + (counts[g][t, 0] > 0.5).astype(jnp.int32)
        idx_ref[g, c] = n_tiles
        n_act.append(c)
    n_trips = jnp.maximum(n_act[0], n_act[1])

    def stage_a(g, tile):
        k0 = pl.multiple_of(tile * kt, kt)
        kk = jnp.concatenate([kslc_ref[0, pl.ds(k0, kt), kcols(g)], e_ref[pl.ds(k0, kt), :]], axis=1)
        s = lax.dot_general(kk, q2[g], NT_DIMS, preferred_element_type=F32)
        s_buf[g] = s
        return jnp.max(s, axis=0, keepdims=True)

    def stage_b(g, tmax, m_i):
        m_new = jnp.maximum(m_i, tmax + cq[g])
        p_buf[g] = jnp.exp2(s_buf[g] - (m_new - cq[g])).astype(BF16)
        return jnp.exp2(m_i - m_new), m_new

    def stage_c(g, tile, alpha, acc):
        k0 = pl.multiple_of(tile * kt, kt)
        pv = jnp.dot(vslct_ref[0, vrows(g), pl.ds(k0, kt)], p_buf[g], preferred_element_type=F32)
        return alpha * acc + pv

    def step(g, i, state, run_a):
        tmax, alpha, m_i, acc = state
        acc = stage_c(g, idx_ref[g, jnp.maximum(i - 2, 0)], alpha, acc)
        alpha, m_i = stage_b(g, tmax, m_i)
        if run_a:
            tmax = stage_a(g, idx_ref[g, i])
        return tmax, alpha, m_i, acc

    states = lax.fori_loop(0, n_trips, lambda i, st: tuple(step(g, i, st[g], True) for g in groups), tuple(states))
    states = tuple(step(g, n_trips, states[g], False) for g in groups)

    out_rows = [None] * NSA_HEADS
    for g in groups:
        _, alpha, _, acc = states[g]
        acc = stage_c(g, idx_ref[g, jnp.maximum(n_trips - 1, 0)], alpha, acc)
        o_s = acc[:HEAD_DIM] * (1.0 / acc[HEAD_DIM:HEAD_DIM + 1])
        for j in range(NSA_GROUP):
            h = g * NSA_GROUP + j
            gt = jax.nn.sigmoid(gate_ref[0, 3 * h:3 * h + 3, :])
            sl = slice(j * tq, (j + 1) * tq)
            out_rows[h] = gt[0:1] * o_c[g][:, sl] + gt[1:2] * o_s[:, sl] + gt[2:3] * o_w[g][:, sl]
    o_ref[0] = jnp.concatenate(out_rows, axis=0).T.astype(o_ref.dtype)


def _nsa_attention(nq, kc, vct, kslc, vslct, kwin, vwint, gates_t):
    b, s, _ = nq.shape
    n_cmp = s // CMP_BLOCK
    n_sel = s // SEL_BLOCK
    n_tiles = s // NSA_KT
    assert SEL_TOPK < n_sel <= LANES and s % NSA_KT == 0 and NSA_TQ <= WINDOW
    pad = NSA_KT
    kslc = jnp.pad(kslc, ((0, 0), (0, pad), (0, 0)))
    vslct = jnp.pad(vslct, ((0, 0), (0, 0), (0, pad)))
    e_sel = jnp.arange(s)[:, None] // SEL_BLOCK == jnp.arange(LANES)[None, :]
    e_sel = jnp.concatenate([e_sel, jnp.ones((pad, LANES), bool)], axis=0).astype(BF16)
    pad_rows = jnp.zeros((WINDOW, LANES), BF16).at[:, PAD_LANE].set(1)
    pad_rows = jnp.broadcast_to(jnp.tile(pad_rows, (1, NSA_KV_HEADS))[None], (b, WINDOW, NSA_K_COLS))
    kwin = jnp.concatenate([pad_rows, kwin], axis=1)
    vwint = jnp.pad(vwint, ((0, 0), (0, 0), (WINDOW, 0)))
    vr = NSA_KV_HEADS * VROWS
    per_b = lambda shape: pl.BlockSpec((1,) + shape, lambda bi, i: (bi, 0, 0))
    wide = NSA_GROUP * NSA_TQ
    return pl.pallas_call(
        functools.partial(_nsa_kernel, seq_len=s),
        grid=(b, s // NSA_TQ),
        in_specs=[pl.BlockSpec((1, NSA_TQ, NSA_Q_COLS), lambda bi, i: (bi, i, 0)),
                  per_b((n_cmp, NSA_K_COLS)), per_b((LANES, n_cmp)),
                  per_b((s + pad, NSA_K_COLS)), per_b((vr, s + pad)),
                  per_b((s + WINDOW, NSA_K_COLS)), per_b((vr, s + WINDOW)),
                  pl.BlockSpec((1, gates_t.shape[1], NSA_TQ), lambda bi, i: (bi, 0, i)),
                  pl.BlockSpec(e_sel.shape, lambda bi, i: (0, 0))],
        out_specs=pl.BlockSpec((1, NSA_TQ, NSA_WIDTH), lambda bi, i: (bi, i, 0)),
        out_shape=jax.ShapeDtypeStruct((b, s, NSA_WIDTH), BF16),
        scratch_shapes=[pltpu.VMEM((NSA_KV_HEADS, NSA_KT, wide), F32), pltpu.VMEM((NSA_KV_HEADS, NSA_KT, wide), BF16),
                        pltpu.SMEM((NSA_KV_HEADS, n_tiles + 1), jnp.int32)],
        compiler_params=_cparams("parallel", "arbitrary"),
        name="nsa_attention",
    )(nq, kc, vct, kslc, vslct, kwin, vwint, gates_t, e_sel)


def _cumsum_kernel(f_ref, b_ref, c_ref, piece_ref):
    s = f_ref.shape[-1]
    ri = lax.broadcasted_iota(jnp.int32, (LANES, LANES), 0)
    ci = lax.broadcasted_iota(jnp.int32, (LANES, LANES), 1)
    upper = (ri <= ci).astype(F32)
    ones = jnp.ones((LANES, LANES), F32)

    def chunk(c, carry):
        c0 = pl.multiple_of(c * LANES, LANES)
        lf = jax.nn.log_sigmoid(f_ref[0, :, pl.ds(c0, LANES)] + b_ref[...])
        within = jnp.dot(lf, upper, preferred_element_type=F32, precision=lax.Precision.HIGHEST)
        total = jnp.dot(lf, ones, preferred_element_type=F32, precision=lax.Precision.HIGHEST)
        c = (within + carry) * LOG2E
        c_ref[0, :, pl.ds(c0, LANES)] = c
        rest = -c
        for i in range(BIAS_PIECES):
            piece = rest.astype(BF16).astype(F32)
            piece_ref[0, i, :, pl.ds(c0, LANES)] = piece
            rest = rest - piece
        return carry + total

    lax.fori_loop(0, s // LANES, chunk, jnp.zeros((f_ref.shape[1], LANES), F32))


def _fox_cumsum(f_t, b_f):
    b, r, s = f_t.shape
    bias = jnp.broadcast_to(jnp.pad(b_f, (0, r - b_f.shape[0]))[:, None], (r, LANES)).astype(F32)
    return pl.pallas_call(
        _cumsum_kernel,
        grid=(b,),
        in_specs=[pl.BlockSpec((1, r, s), lambda i: (i, 0, 0)),
                  pl.BlockSpec((r, LANES), lambda i: (0, 0))],
        out_specs=[pl.BlockSpec((1, r, s), lambda i: (i, 0, 0)),
                   pl.BlockSpec((1, BIAS_PIECES, r, s), lambda i: (i, 0, 0, 0))],
        out_shape=[jax.ShapeDtypeStruct((b, r, s), F32),
                   jax.ShapeDtypeStruct((b, BIAS_PIECES, r, s), F32)],
        compiler_params=_cparams("parallel"),
        name="fox_cumsum",
    )(f_t, bias)


FOX_TQ = 1024
FOX_KT = 1024


def _fox_kernel(q_ref, k_ref, kb_ref, vt_ref, cq_ref, o_ref, s_buf, p_buf):
    tq, kt, vr = FOX_TQ, FOX_KT, VROWS
    q0 = pl.program_id(2) * tq
    n_full = q0 // kt
    lane = lax.broadcasted_iota(jnp.int32, (tq, LANES), 1)
    t_row = q0 + lax.broadcasted_iota(jnp.int32, (1, tq), 1)
    qh = []
    for hh in range(2):
        pick = (lane >= hh * BIAS_PIECES) & (lane < (hh + 1) * BIAS_PIECES)
        qh.append(jnp.concatenate([jnp.where(lane // HEAD_DIM == hh, q_ref[0], 0).astype(BF16),
                                   jnp.where(pick, 1.0, 0.0).astype(BF16)], axis=1))
    cq = [cq_ref[0, hh] for hh in range(2)]

    def stage_a(hh, j, masked):
        k0 = pl.multiple_of(j * kt, kt)
        kk = jnp.concatenate([k_ref[0, pl.ds(k0, kt), :], kb_ref[0, 0, pl.ds(k0, kt), :]], axis=1)
        s = lax.dot_general(kk, qh[hh], NT_DIMS, preferred_element_type=F32)
        if masked:
            s = jnp.where(t_row >= k0 + lax.broadcasted_iota(jnp.int32, (kt, tq), 0), s, -jnp.inf)
        s_buf[hh] = s
        return jnp.max(s, axis=0, keepdims=True)

    def stage_b(hh, tmax, m_i):
        m_new = jnp.maximum(m_i, tmax + cq[hh])
        p_buf[hh] = jnp.exp2(s_buf[hh] - (m_new - cq[hh])).astype(BF16)
        return jnp.exp2(m_i - m_new), m_new

    def stage_c(hh, j, alpha, acc):
        k0 = pl.multiple_of(j * kt, kt)
        pv = jnp.dot(vt_ref[0, hh * vr:(hh + 1) * vr, pl.ds(k0, kt)], p_buf[hh], preferred_element_type=F32)
        return alpha * acc + pv

    def step(hh, i, state, run_a):
        tmax, alpha, m_i, acc = state
        acc = stage_c(hh, jnp.where(i == 1, n_full, jnp.maximum(i - 2, 0)), alpha, acc)
        alpha, m_i = stage_b(hh, tmax, m_i)
        if run_a:
            tmax = stage_a(hh, i, False)
        return tmax, alpha, m_i, acc

    states = []
    for hh in range(2):
        tmax = stage_a(hh, n_full, True)
        p_buf[hh] = jnp.zeros((kt, tq), BF16)
        states.append((tmax, jnp.ones((1, tq), F32), jnp.full((1, tq), NEG_INF, F32), jnp.zeros((vr, tq), F32)))
    states = lax.fori_loop(0, n_full, lambda i, st: tuple(step(hh, i, st[hh], True) for hh in range(2)),
                           tuple(states))
    states = tuple(step(hh, n_full, states[hh], False) for hh in range(2))
    outs = []
    for hh in range(2):
        _, alpha, _, acc = states[hh]
        acc = stage_c(hh, jnp.maximum(n_full - 1, 0), alpha, acc)
        outs.append(acc[:HEAD_DIM] / acc[HEAD_DIM:HEAD_DIM + 1])
    o_ref[0] = jnp.concatenate(outs, axis=0).T.astype(o_ref.dtype)


def _fox_attention(fq, fk, kb, fvt, c_rows):
    b, s, _ = fq.shape
    n_pair = FOX_HEADS // 2
    return pl.pallas_call(
        _fox_kernel,
        grid=(b, n_pair, s // FOX_TQ),
        in_specs=[pl.BlockSpec((1, FOX_TQ, LANES), lambda bi, p, i: (bi, i, p)),
                  pl.BlockSpec((1, s, LANES), lambda bi, p, i: (bi, 0, p)),
                  pl.BlockSpec((1, 1, s, LANES), lambda bi, p, i: (bi, p, 0, 0)),
                  pl.BlockSpec((1, 2 * VROWS, s), lambda bi, p, i: (bi, p, 0)),
                  pl.BlockSpec((1, 2, 1, FOX_TQ), lambda bi, p, i: (bi, p, 0, i))],
        out_specs=pl.BlockSpec((1, FOX_TQ, LANES), lambda bi, p, i: (bi, i, p)),
        out_shape=jax.ShapeDtypeStruct((b, s, FOX_WIDTH), BF16),
        scratch_shapes=[pltpu.VMEM((2, FOX_KT, FOX_TQ), F32), pltpu.VMEM((2, FOX_KT, FOX_TQ), BF16)],
        compiler_params=_cparams("parallel", "parallel", "arbitrary"),
        name="fox_attention",
    )(fq, fk, kb, fvt, c_rows)


CONV_TS = 512
CONV_HALO = 32


def _conv_kernel(u_ref, uh_ref, w_ref, b_ref, g_ref, be_ref, o_ref, y_ref, *, ksize):
    ts, halo, cw = CONV_TS, CONV_HALO, CONV_WIDTH

    def glu(u):
        return u[:, :cw] * jax.nn.sigmoid(u[:, cw:])

    first = pl.program_id(1) == 0
    y_ref[0:halo, :] = jnp.where(first, 0.0, glu(uh_ref[0]))
    y_ref[halo:halo + ts, :] = glu(u_ref[0])
    acc = jnp.zeros((ts, cw), F32)
    for k in range(ksize):
        start = halo - (ksize - 1) + k
        acc = acc + w_ref[k:k + 1, :] * y_ref[start:start + ts, :]
    hn = _ln_rows(acc + b_ref[...], g_ref[...], be_ref[...])
    o_ref[0] = jax.nn.silu(hn).astype(o_ref.dtype)


def _conv_mixer(u, w_dw, b_dw, ln_g, ln_b):
    b, s, _ = u.shape
    ksize = w_dw.shape[0]
    assert ksize - 1 <= CONV_HALO
    ratio = CONV_TS // CONV_HALO
    vec = lambda a: a.reshape(1, CONV_WIDTH)
    const = lambda shape: pl.BlockSpec(shape, lambda bi, i: (0, 0))
    return pl.pallas_call(
        functools.partial(_conv_kernel, ksize=ksize),
        grid=(b, s // CONV_TS),
        in_specs=[pl.BlockSpec((1, CONV_TS, 2 * CONV_WIDTH), lambda bi, i: (bi, i, 0)),
                  pl.BlockSpec((1, CONV_HALO, 2 * CONV_WIDTH), lambda bi, i: (bi, jnp.maximum(i * ratio - 1, 0), 0)),
                  const((ksize, CONV_WIDTH)), const((1, CONV_WIDTH)), const((1, CONV_WIDTH)),
                  const((1, CONV_WIDTH))],
        out_specs=pl.BlockSpec((1, CONV_TS, CONV_WIDTH), lambda bi, i: (bi, i, 0)),
        out_shape=jax.ShapeDtypeStruct((b, s, CONV_WIDTH), BF16),
        scratch_shapes=[pltpu.VMEM((CONV_HALO + CONV_TS, CONV_WIDTH), F32)],
        compiler_params=_cparams("parallel", "arbitrary"),
        name="conv_mixer",
    )(u, u, w_dw, vec(b_dw), vec(ln_g), vec(ln_b))


def _outproj_kernel(oa_ref, ob_ref, oc_ref, x_ref, wa_ref, wb_ref, wc_ref, g_ref, b_ref, o_ref, *, alpha):
    mix = jnp.dot(oa_ref[...], wa_ref[...], preferred_element_type=F32)
    mix = mix + jnp.dot(ob_ref[...], wb_ref[...], preferred_element_type=F32)
    mix = mix + jnp.dot(oc_ref[...], wc_ref[...], preferred_element_type=F32)
    o_ref[...] = _ln_rows(alpha * x_ref[...] + mix, g_ref[...], b_ref[...])


def _out_proj(o_a, o_b, o_c, x2d, w_out, g, b, alpha, tm=512):
    t, d = x2d.shape
    wa = w_out[:NSA_WIDTH].astype(BF16)
    wb = w_out[NSA_WIDTH:NSA_WIDTH + CONV_WIDTH].astype(BF16)
    wc = w_out[NSA_WIDTH + CONV_WIDTH:].astype(BF16)
    rows = lambda w: pl.BlockSpec((tm, w), lambda i: (i, 0))
    full = lambda a: pl.BlockSpec(a.shape, lambda i: (0, 0))
    return pl.pallas_call(
        functools.partial(_outproj_kernel, alpha=alpha),
        grid=(t // tm,),
        in_specs=[rows(NSA_WIDTH), rows(CONV_WIDTH), rows(FOX_WIDTH), rows(d), full(wa), full(wb), full(wc),
                  pl.BlockSpec((1, d), lambda i: (0, 0)), pl.BlockSpec((1, d), lambda i: (0, 0))],
        out_specs=rows(d),
        out_shape=jax.ShapeDtypeStruct((t, d), F32),
        compiler_params=_cparams("parallel"),
        name="out_proj_ln",
    )(o_a, o_b, o_c, x2d, wa, wb, wc, g.reshape(1, d), b.reshape(1, d))


FFN_TM = 1024
FFN_HALO = 8
FFN_FC = 256


def _ffn_kernel(x_ref, xh_ref, wg_ref, wu_ref, wd_ref, cw_ref, cb_ref, g_ref, b_ref, o_ref, gate_ref, h_ref,
                *, alpha, n_chunks, ksize):
    tm, halo, fc = FFN_TM, FFN_HALO, FFN_FC
    x = x_ref[0]
    xb = x.astype(BF16)
    xhb = xh_ref[0].astype(BF16)
    first = pl.program_id(1) == 0
    for c in range(n_chunks):
        cs = slice(c * fc, (c + 1) * fc)
        gate_ref[0:halo, :] = jnp.where(first, 0.0, jnp.dot(xhb, wg_ref[:, cs], preferred_element_type=F32))
        gate_ref[halo:halo + tm, :] = jnp.dot(xb, wg_ref[:, cs], preferred_element_type=F32)
        conv = cb_ref[:, cs]
        for k in range(ksize):
            start = halo - (ksize - 1) + k
            conv = conv + cw_ref[k:k + 1, cs] * gate_ref[start:start + tm, :]
        up = jnp.dot(xb, wu_ref[:, cs], preferred_element_type=F32)
        h_ref[:, cs] = (jax.nn.silu(conv) * up).astype(BF16)
    down = jnp.dot(h_ref[...], wd_ref[...], preferred_element_type=F32)
    o_ref[0] = _ln_rows(alpha * x + down, g_ref[...], b_ref[...])


def _conv_ffn(x3d, w_ffn_in, w_conv, b_conv, w_down, g, b, alpha):
    bsz, s, d = x3d.shape
    d_ff = w_down.shape[0]
    ksize = w_conv.shape[0]
    assert d_ff % FFN_FC == 0 and ksize - 1 <= FFN_HALO
    wg = w_ffn_in[:, :d_ff].astype(BF16)
    wu = w_ffn_in[:, d_ff:].astype(BF16)
    wd = w_down.astype(BF16)
    ratio = FFN_TM // FFN_HALO
    const = lambda shape: pl.BlockSpec(shape, lambda bi, i: (0, 0))
    return pl.pallas_call(
        functools.partial(_ffn_kernel, alpha=alpha, n_chunks=d_ff // FFN_FC, ksize=ksize),
        grid=(bsz, s // FFN_TM),
        in_specs=[pl.BlockSpec((1, FFN_TM, d), lambda bi, i: (bi, i, 0)),
                  pl.BlockSpec((1, FFN_HALO, d), lambda bi, i: (bi, jnp.maximum(i * ratio - 1, 0), 0)),
                  const((d, d_ff)), const((d, d_ff)), const((d_ff, d)),
                  const((ksize, d_ff)), const((1, d_ff)), const((1, d)), const((1, d))],
        out_specs=pl.BlockSpec((1, FFN_TM, d), lambda bi, i: (bi, i, 0)),
        out_shape=jax.ShapeDtypeStruct((bsz, s, d), F32),
        scratch_shapes=[pltpu.VMEM((FFN_HALO + FFN_TM, FFN_FC), F32), pltpu.VMEM((FFN_TM, d_ff), BF16)],
        compiler_params=_cparams("parallel", "arbitrary"),
        name="conv_ffn_ln",
    )(x3d, x3d, wg, wu, wd, w_conv, b_conv.reshape(1, d_ff), g.reshape(1, d), b.reshape(1, d))


def kernel(x, ln_emb_g, ln_emb_b, w_in, b_f, w_cmp1, w_cmp2, pe_cmp, w_dw, b_dw, ln_conv_g, ln_conv_b, w_out,
           ln1_g, ln1_b, w_ffn_in, w_ffn_conv, b_ffn_conv, w_ffn_down, ln2_g, ln2_b):
    bsz, s, d = x.shape
    depth = w_in.shape[0]
    alpha = (2.0 * depth) ** 0.25
    t = bsz * s

    xc = _layer_norm(x.reshape(t, d), ln_emb_g, ln_emb_b)
    for l in range(depth):
        (nq, kslc, kwin, fq, fk, cmp_k, cmp_v, conv_u, vslct, vwint, fvt, gates_t, f_t) = _project(
            xc.reshape(bsz, s, d), *_split_w_in(w_in[l]))

        kc, vct = _compress(cmp_k, cmp_v, pe_cmp[l], w_cmp1[l], w_cmp2[l])
        o_a = _nsa_attention(nq, kc, vct, kslc, vslct, kwin, vwint, gates_t)

        o_b = _conv_mixer(conv_u, w_dw[l], b_dw[l], ln_conv_g[l], ln_conv_b[l])

        c, pieces = _fox_cumsum(f_t, b_f[l])
        c_rows = c[:, :FOX_HEADS].reshape(bsz, FOX_HEADS, 1, s)
        kb = pieces[:, :, :FOX_HEADS].reshape(bsz, BIAS_PIECES, FOX_HEADS // 2, 2, s)
        kb = kb.transpose(0, 2, 4, 3, 1).reshape(bsz, FOX_HEADS // 2, s, 2 * BIAS_PIECES)
        kb = jnp.pad(kb, ((0, 0), (0, 0), (0, 0), (0, LANES - 2 * BIAS_PIECES))).astype(BF16)
        o_c = _fox_attention(fq, fk, kb, fvt, c_rows)

        xc = _out_proj(o_a.reshape(t, NSA_WIDTH), o_b.reshape(t, CONV_WIDTH), o_c.reshape(t, FOX_WIDTH),
                       xc, w_out[l], ln1_g[l], ln1_b[l], alpha)
        xc = _conv_ffn(xc.reshape(bsz, s, d), w_ffn_in[l], w_ffn_conv[l], b_ffn_conv[l], w_ffn_down[l],
                       ln2_g[l], ln2_b[l], alpha).reshape(t, d)
    return xc.reshape(bsz, s, d)
```

```python
import functools

import jax
import jax.numpy as jnp
import numpy as np
from jax import lax
from jax.experimental import pallas as pl
from jax.experimental.pallas import tpu as pltpu

HEAD_DIM = 64
NSA_HEADS = 6
NSA_KV_HEADS = 2
NSA_GROUP = 3
FOX_HEADS = 6
NSA_WIDTH = NSA_HEADS * HEAD_DIM
CONV_WIDTH = 256
FOX_WIDTH = FOX_HEADS * HEAD_DIM
CMP_BLOCK = 32
SEL_BLOCK = 64
SEL_TOPK = 16
WINDOW = 512
LN_EPS = 1e-5
NEG_INF = -1e30
SCALE = HEAD_DIM ** -0.5
LOG2E = 1.4426950408889634
BIAS_PIECES = 3
BIAS_LANE0 = HEAD_DIM
PAD_LANE = BIAS_LANE0 + 2 * BIAS_PIECES
VROWS = HEAD_DIM + 16
MASK_BIG = 2.0 ** 100

LANES = 128
VMEM_LIMIT = 56 * 1024 * 1024

BF16 = jnp.bfloat16
F32 = jnp.float32
NT_DIMS = (((1,), (1,)), ((), ()))


def _cparams(*sem):
    return pltpu.CompilerParams(dimension_semantics=sem, vmem_limit_bytes=VMEM_LIMIT)


def _ln_rows(x, g, b):
    mu = jnp.mean(x, axis=-1, keepdims=True)
    xc = x - mu
    var = jnp.mean(xc * xc, axis=-1, keepdims=True)
    return xc * lax.rsqrt(var + LN_EPS) * g + b


def _bf16_pieces(value, n=BIAS_PIECES):
    out = []
    rest = np.float32(value)
    for _ in range(n):
        bits = np.float32(rest).view(np.uint32)
        bits = (bits + np.uint32(0x7FFF) + ((bits >> np.uint32(16)) & np.uint32(1))) & np.uint32(0xFFFF0000)
        piece = bits.view(np.float32)
        out.append(float(piece))
        rest = np.float32(rest - piece)
    return out


def _position_lanes(pos):
    hi = ((pos // SEL_BLOCK) * SEL_BLOCK).astype(F32)
    lo = (pos % SEL_BLOCK).astype(F32)
    lanes = jnp.zeros((pos.shape[0], LANES), F32)
    for i in range(BIAS_PIECES):
        lanes = lanes.at[:, BIAS_LANE0 + 2 * i].set(hi)
        lanes = lanes.at[:, BIAS_LANE0 + 2 * i + 1].set(lo)
    return lanes


NSA_Q_COLS = NSA_HEADS * LANES
NSA_K_COLS = NSA_KV_HEADS * LANES
ROWS_SPLIT = (NSA_Q_COLS, NSA_K_COLS, NSA_K_COLS, FOX_WIDTH, FOX_WIDTH)
F32_SPLIT = (128, 128, 2 * CONV_WIDTH)
T_SPLIT = (128, 128, FOX_WIDTH, 24, 8)
T_ROWS = tuple(w // HEAD_DIM * VROWS for w in T_SPLIT[:3]) + T_SPLIT[3:]


def _proj_kernel(x_ref, g_ref, b_ref, wr_ref, wf_ref, wt_ref, pos_ref,
                 nq_ref, kslc_ref, kwin_ref, fq_ref, fk_ref, cmpk_ref, cmpv_ref, conv_ref,
                 vslc_ref, vwin_ref, fv_ref, gate_ref, flog_ref, *xn_ref):
    x = x_ref[0]
    if xn_ref:
        x = _ln_rows(x, g_ref[...], b_ref[...])
        xn_ref[0][0] = x
    xb = x.astype(BF16)
    rows = jnp.dot(xb, wr_ref[...], preferred_element_type=F32)
    off = 0
    for ref, w in zip((nq_ref, kslc_ref, kwin_ref, fq_ref, fk_ref), ROWS_SPLIT):
        blk = rows[:, off:off + w]
        if ref is kslc_ref or ref is kwin_ref:
            blk = blk + pos_ref[...]
        ref[0] = blk.astype(ref.dtype)
        off += w
    f32o = jnp.dot(xb, wf_ref[...], preferred_element_type=F32)
    off = 0
    for ref, w in zip((cmpk_ref, cmpv_ref, conv_ref), F32_SPLIT):
        ref[0] = f32o[:, off:off + w]
        off += w
    tr = lax.dot_general(wt_ref[...], xb, NT_DIMS, preferred_element_type=F32)
    ones = jnp.ones((VROWS - HEAD_DIM, tr.shape[1]), F32)
    off = 0
    for ref, w in zip((vslc_ref, vwin_ref, fv_ref, gate_ref, flog_ref), T_SPLIT):
        blk = tr[off:off + w, :]
        if ref.shape[1] != w:
            blk = jnp.concatenate([part for h in range(w // HEAD_DIM)
                                   for part in (blk[h * HEAD_DIM:(h + 1) * HEAD_DIM], ones)], axis=0)
        ref[0] = blk.astype(ref.dtype)
        off += w


def _split_w_in(w_in_l):
    kvw = NSA_KV_HEADS * HEAD_DIM
    o = 0
    w_nq = w_in_l[:, o:o + NSA_WIDTH]; o += NSA_WIDTH
    w_kv = w_in_l[:, o:o + 6 * kvw]; o += 6 * kvw
    w_g = w_in_l[:, o:o + 3 * NSA_HEADS]; o += 3 * NSA_HEADS
    w_conv = w_in_l[:, o:o + 2 * CONV_WIDTH]; o += 2 * CONV_WIDTH
    w_fqkv = w_in_l[:, o:o + 3 * FOX_WIDTH]; o += 3 * FOX_WIDTH
    w_f = w_in_l[:, o:o + FOX_HEADS]
    d = w_in_l.shape[0]

    def head_blocks(w, n_heads):
        w = w.reshape(d, n_heads, HEAD_DIM)
        return jnp.pad(w, ((0, 0), (0, 0), (0, LANES - HEAD_DIM))).reshape(d, n_heads * LANES)

    w_nq = head_blocks(w_nq * (SCALE * LOG2E), NSA_HEADS)
    kv = [w_kv[:, i * kvw:(i + 1) * kvw] for i in range(6)]
    w_fq = w_fqkv[:, :FOX_WIDTH] * (SCALE * LOG2E)
    w_fk = w_fqkv[:, FOX_WIDTH:2 * FOX_WIDTH]
    w_fv = w_fqkv[:, 2 * FOX_WIDTH:]
    w_rows = jnp.concatenate([w_nq, head_blocks(kv[2], NSA_KV_HEADS), head_blocks(kv[4], NSA_KV_HEADS),
                              w_fq, w_fk], axis=1).astype(BF16)
    w_f32 = jnp.concatenate([kv[0], kv[1], w_conv], axis=1).astype(BF16)
    w_gp = jnp.pad(w_g, ((0, 0), (0, T_SPLIT[3] - w_g.shape[1])))
    w_fp = jnp.pad(w_f, ((0, 0), (0, T_SPLIT[4] - w_f.shape[1])))
    w_t = jnp.concatenate([kv[3], kv[5], w_fv, w_gp, w_fp], axis=1).T.astype(BF16)
    return w_rows, w_f32, w_t


def _project(x3d, ln_g, ln_b, w_rows, w_f32, w_t, apply_ln, tm=512):
    b, s, d = x3d.shape
    pos_lanes = jnp.tile(_position_lanes(jnp.arange(s)), (1, NSA_KV_HEADS))
    row_spec = lambda w: pl.BlockSpec((1, tm, w), lambda bi, i: (bi, i, 0))
    t_spec = lambda w: pl.BlockSpec((1, w, tm), lambda bi, i: (bi, 0, i))
    full = lambda a: pl.BlockSpec(a.shape, lambda bi, i: (0, 0))
    out_shape = ([jax.ShapeDtypeStruct((b, s, w), BF16) for w in ROWS_SPLIT]
                 + [jax.ShapeDtypeStruct((b, s, w), F32) for w in F32_SPLIT]
                 + [jax.ShapeDtypeStruct((b, w, s), dt) for w, dt in zip(T_ROWS, (BF16, BF16, BF16, F32, F32))])
    out_specs = [row_spec(w) for w in ROWS_SPLIT + F32_SPLIT] + [t_spec(w) for w in T_ROWS]
    if apply_ln:
        out_shape.append(jax.ShapeDtypeStruct((b, s, d), F32))
        out_specs.append(row_spec(d))
    vec = pl.BlockSpec((1, d), lambda bi, i: (0, 0))
    return pl.pallas_call(
        _proj_kernel,
        grid=(b, s // tm),
        in_specs=[row_spec(d), vec, vec, full(w_rows), full(w_f32), full(w_t),
                  pl.BlockSpec((tm, NSA_K_COLS), lambda bi, i: (i, 0))],
        out_specs=out_specs,
        out_shape=out_shape,
        compiler_params=_cparams("parallel", "parallel"),
        name="in_proj",
    )(x3d, ln_g.reshape(1, d), ln_b.reshape(1, d), w_rows, w_f32, w_t, pos_lanes)


def _compress_kernel(rawk_ref, rawv_ref, pe_ref, w1_ref, w2k_ref, w2vt_ref, end_ref, kc_ref, vct_ref):
    s = rawk_ref.shape[1]
    half = s // (2 * CMP_BLOCK)
    hidden = w1_ref.shape[-1] // NSA_KV_HEADS
    outs = []
    for i, raw_ref in enumerate((rawk_ref, rawv_ref)):
        hid = jnp.zeros((2 * half, w1_ref.shape[-1]), F32)
        for p in range(CMP_BLOCK):
            rows = jnp.concatenate(
                [raw_ref[0, pl.ds(par * CMP_BLOCK + p, half, stride=2 * CMP_BLOCK), :]
                 for par in range(2)], axis=0)
            hid = hid + jnp.dot((rows + pe_ref[i, p:p + 1, :]).astype(BF16), w1_ref[i, p],
                                preferred_element_type=F32)
        outs.append(jax.nn.gelu(hid).astype(BF16))
    kc_ref[0] = jnp.concatenate(
        [jnp.dot(outs[0][:, g * hidden:(g + 1) * hidden], w2k_ref[...], preferred_element_type=F32) + end_ref[...]
         for g in range(NSA_KV_HEADS)], axis=1).astype(kc_ref.dtype)
    vct_ref[0] = jnp.concatenate(
        [lax.dot_general(w2vt_ref[...], outs[1][:, g * hidden:(g + 1) * hidden], NT_DIMS,
                         preferred_element_type=F32) for g in range(NSA_KV_HEADS)], axis=0).astype(vct_ref.dtype)


def _compress(cmp_k, cmp_v, pe, w1, w2):
    b, s, _ = cmp_k.shape
    n_cmp = s // CMP_BLOCK
    hidden = w1.shape[-1]
    pe2 = jnp.tile(pe, (1, 1, NSA_KV_HEADS))
    w1p = w1.reshape(2, CMP_BLOCK, HEAD_DIM, hidden)
    zero = jnp.zeros_like(w1p)
    w1blk = jnp.concatenate([jnp.concatenate([w1p, zero], axis=-1),
                             jnp.concatenate([zero, w1p], axis=-1)], axis=2).astype(BF16)
    w2k = jnp.pad(w2[0], ((0, 0), (0, LANES - HEAD_DIM))).astype(BF16)
    w2vt = w2[1].T.astype(BF16)
    blk = jnp.concatenate([jnp.arange(0, n_cmp, 2), jnp.arange(1, n_cmp, 2)])
    end_lanes = _position_lanes(blk * CMP_BLOCK + CMP_BLOCK - 1)
    const = lambda a: pl.BlockSpec(a.shape, lambda i: (0,) * a.ndim)
    return pl.pallas_call(
        _compress_kernel,
        grid=(b,),
        in_specs=[pl.BlockSpec((1, s, LANES), lambda i: (i, 0, 0)), pl.BlockSpec((1, s, LANES), lambda i: (i, 0, 0)),
                  const(pe2), const(w1blk), const(w2k), const(w2vt), const(end_lanes)],
        out_specs=[pl.BlockSpec((1, n_cmp, NSA_K_COLS), lambda i: (i, 0, 0)),
                   pl.BlockSpec((1, NSA_KV_HEADS * HEAD_DIM, n_cmp), lambda i: (i, 0, 0))],
        out_shape=[jax.ShapeDtypeStruct((b, n_cmp, NSA_K_COLS), BF16),
                   jax.ShapeDtypeStruct((b, NSA_KV_HEADS * HEAD_DIM, n_cmp), BF16)],
        compiler_params=_cparams("parallel"),
        name="nsa_compress",
    )(cmp_k, cmp_v, pe2, w1blk, w2k, w2vt, end_lanes)


NSA_TQ = 256
NSA_KT = 512


def _alibi_slope(h):
    return float(2.0 ** (-8.0 * (h + 1) / NSA_HEADS))


def _nsa_kernel(q_ref, kc_ref, vct_ref, kslc_ref, vslct_ref, kwin_ref, vwint_ref, gate_ref, e_ref, o_ref,
                s_buf, p_buf, idx_ref, *, seq_len):
    tq, kt = NSA_TQ, NSA_KT
    n_cmp = seq_len // CMP_BLOCK
    n_sel = seq_len // SEL_BLOCK
    n_tiles = seq_len // kt
    top_k = min(SEL_TOPK, n_sel)
    kw = WINDOW + tq
    wide = NSA_GROUP * tq
    qi = pl.program_id(1)
    q0 = qi * tq
    t_row = q0 + lax.broadcasted_iota(jnp.int32, (1, tq), 1)
    t_f = t_row.astype(F32)
    lane = lax.broadcasted_iota(jnp.int32, (1, LANES), 1)
    cur = t_row // SEL_BLOCK
    groups = range(NSA_KV_HEADS)

    def tile3(a):
        return jnp.concatenate([a] * NSA_GROUP, axis=1)

    q1, cq = [], []
    for g in groups:
        rows, consts = [], []
        for j in range(NSA_GROUP):
            h = g * NSA_GROUP + j
            slope2 = _alibi_slope(h) * LOG2E
            sl = jnp.where(lane == PAD_LANE, -MASK_BIG, jnp.zeros((1, LANES), F32))
            for i, piece in enumerate(_bf16_pieces(slope2)):
                sl = jnp.where((lane == BIAS_LANE0 + 2 * i) | (lane == BIAS_LANE0 + 2 * i + 1), piece, sl)
            rows.append(q_ref[0, :, h * LANES:(h + 1) * LANES] + sl.astype(BF16))
            consts.append(-slope2 * t_f)
        q1.append(jnp.concatenate(rows, axis=0))
        cq.append(jnp.concatenate(consts, axis=1))

    def softmax_weights(s, valid):
        ps = []
        for j in range(NSA_GROUP):
            sj = jnp.where(valid, s[:, j * tq:(j + 1) * tq], NEG_INF)
            m = jnp.max(sj, axis=0, keepdims=True)
            p = jnp.exp2(sj - m)
            l = jnp.sum(p, axis=0, keepdims=True)
            ps.append(p * jnp.where(m > 0.5 * NEG_INF, 1.0 / l, 0.0))
        return jnp.concatenate(ps, axis=1)

    def softmax_out(s, vt):
        m = jnp.max(s, axis=0, keepdims=True)
        pv = jnp.dot(vt, jnp.exp2(s - m).astype(BF16), preferred_element_type=F32)
        return pv[:HEAD_DIM] * (1.0 / pv[HEAD_DIM:HEAD_DIM + 1])

    def vrows(g):
        return slice(g * VROWS, (g + 1) * VROWS)

    def kcols(g):
        return slice(g * LANES, (g + 1) * LANES)

    r = lax.broadcasted_iota(jnp.int32, (n_cmp, tq), 0)
    blk_id = jnp.where(r < n_sel, 2 * r, 2 * (r - n_sel) + 1)
    valid_c = t_row - (blk_id * CMP_BLOCK + CMP_BLOCK - 1) >= 0
    o_c, score = [], []
    jid = lax.broadcasted_iota(jnp.int32, (n_sel, tq), 0)
    forced = (jid == 0) | (jid == cur)
    for g in groups:
        s_c = lax.dot_general(kc_ref[0, :, kcols(g)], q1[g], NT_DIMS, preferred_element_type=F32)
        p_c = softmax_weights(s_c, valid_c)
        o_c.append(jnp.dot(vct_ref[0, g * HEAD_DIM:(g + 1) * HEAD_DIM, :], p_c.astype(BF16),
                           preferred_element_type=F32))
        imp = p_c[:, 0:tq] + p_c[:, tq:2 * tq] + p_c[:, 2 * tq:3 * tq]
        imp = imp[0:n_sel, :] + imp[n_sel:2 * n_sel, :]
        score.append(jnp.where(jid > cur, -1.0, jnp.where(forced, NSA_GROUP + 1.0, imp)))

    for _ in range(top_k):
        for g in groups:
            mx = jnp.max(score[g], axis=0, keepdims=True)
            first = jnp.min(jnp.where(score[g] == mx, jid, n_sel), axis=0, keepdims=True)
            score[g] = jnp.where(jid == first, -2.0, score[g])
    sel = [jnp.where(score[g] == -2.0, 1.0, 0.0) for g in groups]

    w0 = pl.multiple_of(q0, tq)
    row = lax.broadcasted_iota(jnp.int32, (tq, tq), 0)
    col = lax.broadcasted_iota(jnp.int32, (tq, tq), 1)
    near = tile3(col < row)
    causal = tile3(row <= col)
    o_w = []
    for g in groups:
        s_w = lax.dot_general(kwin_ref[0, pl.ds(w0, kw), kcols(g)], q1[g], NT_DIMS, preferred_element_type=F32)
        s_w = jnp.concatenate([jnp.where(near, s_w[:tq], -jnp.inf), s_w[tq:WINDOW],
                               jnp.where(causal, s_w[WINDOW:], -jnp.inf)], axis=0)
        o_w.append(softmax_out(s_w, vwint_ref[0, vrows(g), pl.ds(w0, kw)]))

    kpos = q0 + lax.broadcasted_iota(jnp.int32, (tq, tq), 0)
    valid_o = tile3((kpos // SEL_BLOCK == cur) & (kpos <= t_row))
    states, q2, counts = [], [], []
    past = jid < cur
    tile_of_blk = (lax.broadcasted_iota(jnp.int32, (max(n_tiles, 8), n_sel), 1) // (kt // SEL_BLOCK)
                   == lax.broadcasted_iota(jnp.int32, (max(n_tiles, 8), n_sel), 0)).astype(BF16)
    for g in groups:
        s = lax.dot_general(kslc_ref[0, pl.ds(q0, tq), kcols(g)], q1[g], NT_DIMS, preferred_element_type=F32)
        s = jnp.where(valid_o, s, -jnp.inf)
        m0 = jnp.max(s, axis=0, keepdims=True) + cq[g]
        p = jnp.exp2(s - (m0 - cq[g]))
        acc0 = jnp.dot(vslct_ref[0, vrows(g), pl.ds(q0, tq)], p.astype(BF16), preferred_element_type=F32)
        p_buf[g] = jnp.zeros((kt, wide), BF16)
        states.append((jnp.ones((1, wide), F32), m0, acc0))

        sel_sw = jnp.where(past, sel[g], 0.0)
        sel_pad = jnp.concatenate([sel_sw, jnp.zeros((LANES - n_sel, tq), F32)], axis=0) if n_sel < LANES else sel_sw
        mask_lanes = ((sel_pad.T - 1.0) * MASK_BIG).astype(BF16)
        q2.append(jnp.concatenate([q1[g], jnp.concatenate([mask_lanes] * NSA_GROUP, axis=0)], axis=1))
        per_tile = jnp.dot(tile_of_blk, sel_sw.astype(BF16), preferred_element_type=F32)
        counts.append(jnp.dot(per_tile.astype(BF16), jnp.ones((tq, LANES), BF16), preferred_element_type=F32))

    n_act = []
    for g in groups:
        for t in range(n_tiles + 1):
            idx_ref[g, t] = n_tiles
        c = jnp.int32(0)
        for t in range(n_tiles):
            idx_ref[g, c] = t
            c = c + (counts[g][t, 0] > 0.5).astype(jnp.int32)
        idx_ref[g, c] = n_tiles
        n_act.append(c)
    n_trips = jnp.maximum(n_act[0], n_act[1])

    def stage_a(g, tile):
        k0 = pl.multiple_of(tile * kt, kt)
        kk = jnp.concatenate([kslc_ref[0, pl.ds(k0, kt), kcols(g)], e_ref[pl.ds(k0, kt), :]], axis=1)
        s = lax.dot_general(kk, q2[g], NT_DIMS, preferred_element_type=F32)
        s_buf[g] = s
        return jnp.max(s, axis=0, keepdims=True)

    def stage_b(g, tmax, m_i):
        m_new = jnp.maximum(m_i, tmax + cq[g])
        p_buf[g] = jnp.exp2(s_buf[g] - (m_new - cq[g])).astype(BF16)
        return jnp.exp2(m_i - m_new), m_new

    def stage_c(g, tile, alpha, acc):
        k0 = pl.multiple_of(tile * kt, kt)
        pv = jnp.dot(vslct_ref[0, vrows(g), pl.ds(k0, kt)], p_buf[g], preferred_element_type=F32)
        return alpha * acc + pv

    def step(g, i, state, run_a):
        tmax, alpha, m_i, acc = state
        acc = stage_c(g, idx_ref[g, jnp.maximum(i - 2, 0)], alpha, acc)
        alpha, m_i = stage_b(g, tmax, m_i)
        if run_a:
            tmax = stage_a(g, idx_ref[g, i])
        return tmax, alpha, m_i, acc

    n_trips = jnp.maximum(n_trips, 1)
    states = tuple((stage_a(g, idx_ref[g, 0]),) + states[g] for g in groups)
    states = lax.fori_loop(1, n_trips, lambda i, st: tuple(step(g, i, st[g], True) for g in groups), states)
    states = tuple(step(g, n_trips, states[g], False) for g in groups)

    out_rows = [None] * NSA_HEADS
    for g in groups:
        _, alpha, _, acc = states[g]
        acc = stage_c(g, idx_ref[g, n_trips - 1], alpha, acc)
        o_s = acc[:HEAD_DIM] * (1.0 / acc[HEAD_DIM:HEAD_DIM + 1])
        for j in range(NSA_GROUP):
            h = g * NSA_GROUP + j
            gt = jax.nn.sigmoid(gate_ref[0, 3 * h:3 * h + 3, :])
            sl = slice(j * tq, (j + 1) * tq)
            out_rows[h] = gt[0:1] * o_c[g][:, sl] + gt[1:2] * o_s[:, sl] + gt[2:3] * o_w[g][:, sl]
    o_ref[0] = jnp.concatenate(out_rows, axis=0).T.astype(o_ref.dtype)


def _nsa_attention(nq, kc, vct, kslc, vslct, kwin, vwint, gates_t):
    b, s, _ = nq.shape
    n_cmp = s // CMP_BLOCK
    n_sel = s // SEL_BLOCK
    n_tiles = s // NSA_KT
    assert SEL_TOPK < n_sel <= LANES and s % NSA_KT == 0 and NSA_TQ <= WINDOW
    pad = NSA_KT
    kslc = jnp.pad(kslc, ((0, 0), (0, pad), (0, 0)))
    vslct = jnp.pad(vslct, ((0, 0), (0, 0), (0, pad)))
    e_sel = jnp.arange(s)[:, None] // SEL_BLOCK == jnp.arange(LANES)[None, :]
    e_sel = jnp.concatenate([e_sel, jnp.ones((pad, LANES), bool)], axis=0).astype(BF16)
    pad_rows = jnp.zeros((WINDOW, LANES), BF16).at[:, PAD_LANE].set(1)
    pad_rows = jnp.broadcast_to(jnp.tile(pad_rows, (1, NSA_KV_HEADS))[None], (b, WINDOW, NSA_K_COLS))
    kwin = jnp.concatenate([pad_rows, kwin], axis=1)
    vwint = jnp.pad(vwint, ((0, 0), (0, 0), (WINDOW, 0)))
    vr = NSA_KV_HEADS * VROWS
    per_b = lambda shape: pl.BlockSpec((1,) + shape, lambda bi, i: (bi, 0, 0))
    wide = NSA_GROUP * NSA_TQ
    return pl.pallas_call(
        functools.partial(_nsa_kernel, seq_len=s),
        grid=(b, s // NSA_TQ),
        in_specs=[pl.BlockSpec((1, NSA_TQ, NSA_Q_COLS), lambda bi, i: (bi, i, 0)),
                  per_b((n_cmp, NSA_K_COLS)), per_b((LANES, n_cmp)),
                  per_b((s + pad, NSA_K_COLS)), per_b((vr, s + pad)),
                  per_b((s + WINDOW, NSA_K_COLS)), per_b((vr, s + WINDOW)),
                  pl.BlockSpec((1, gates_t.shape[1], NSA_TQ), lambda bi, i: (bi, 0, i)),
                  pl.BlockSpec(e_sel.shape, lambda bi, i: (0, 0))],
        out_specs=pl.BlockSpec((1, NSA_TQ, NSA_WIDTH), lambda bi, i: (bi, i, 0)),
        out_shape=jax.ShapeDtypeStruct((b, s, NSA_WIDTH), BF16),
        scratch_shapes=[pltpu.VMEM((NSA_KV_HEADS, NSA_KT, wide), F32), pltpu.VMEM((NSA_KV_HEADS, NSA_KT, wide), BF16),
                        pltpu.SMEM((NSA_KV_HEADS, n_tiles + 1), jnp.int32)],
        compiler_params=_cparams("parallel", "arbitrary"),
        name="nsa_attention",
    )(nq, kc, vct, kslc, vslct, kwin, vwint, gates_t, e_sel)


def _cumsum_kernel(f_ref, b_ref, c_ref, piece_ref):
    rows = f_ref.shape[1]
    per_head = rows // 8
    hi = lax.Precision.HIGHEST
    ri = lax.broadcasted_iota(jnp.int32, (LANES, LANES), 0)
    ci = lax.broadcasted_iota(jnp.int32, (LANES, LANES), 1)
    upper = (ri <= ci).astype(F32)
    rr = lax.broadcasted_iota(jnp.int32, (rows, rows), 0)
    rc = lax.broadcasted_iota(jnp.int32, (rows, rows), 1)
    earlier = ((rc < rr) & (rc // per_head == rr // per_head)).astype(F32)
    lf = jax.nn.log_sigmoid(f_ref[0] + b_ref[...])
    within = jnp.dot(lf, upper, preferred_element_type=F32, precision=hi)
    total = jnp.dot(lf, jnp.ones((LANES, LANES), F32), preferred_element_type=F32, precision=hi)
    carry = jnp.dot(earlier, total, preferred_element_type=F32, precision=hi)
    c = (within + carry) * LOG2E
    c_ref[0] = c
    rest = -c
    for i in range(BIAS_PIECES):
        piece = rest.astype(BF16).astype(F32)
        piece_ref[0, i] = piece
        rest = rest - piece


def _fox_cumsum(f_t, b_f):
    b, r, s = f_t.shape
    assert r == 8
    rows = r * (s // LANES)
    bias = jnp.repeat(jnp.pad(b_f, (0, r - b_f.shape[0])), s // LANES)
    bias = jnp.broadcast_to(bias[:, None], (rows, LANES)).astype(F32)
    c, pieces = pl.pallas_call(
        _cumsum_kernel,
        grid=(b,),
        in_specs=[pl.BlockSpec((1, rows, LANES), lambda i: (i, 0, 0)),
                  pl.BlockSpec((rows, LANES), lambda i: (0, 0))],
        out_specs=[pl.BlockSpec((1, rows, LANES), lambda i: (i, 0, 0)),
                   pl.BlockSpec((1, BIAS_PIECES, rows, LANES), lambda i: (i, 0, 0, 0))],
        out_shape=[jax.ShapeDtypeStruct((b, rows, LANES), F32),
                   jax.ShapeDtypeStruct((b, BIAS_PIECES, rows, LANES), F32)],
        compiler_params=_cparams("parallel"),
        name="fox_cumsum",
    )(f_t.reshape(b, rows, LANES), bias)
    return c.reshape(b, r, s), pieces.reshape(b, BIAS_PIECES, r, s)


FOX_TQ = 1024
FOX_KT = 1024


def _fox_kernel(q_ref, k_ref, kb_ref, vt_ref, cq_ref, o_ref, s_buf, p_buf):
    tq, kt, vr = FOX_TQ, FOX_KT, VROWS
    q0 = pl.program_id(2) * tq
    n_full = q0 // kt
    lane = lax.broadcasted_iota(jnp.int32, (tq, LANES), 1)
    t_row = q0 + lax.broadcasted_iota(jnp.int32, (1, tq), 1)
    qh = []
    for hh in range(2):
        pick = (lane >= hh * BIAS_PIECES) & (lane < (hh + 1) * BIAS_PIECES)
        qh.append(jnp.concatenate([jnp.where(lane // HEAD_DIM == hh, q_ref[0], 0).astype(BF16),
                                   jnp.where(pick, 1.0, 0.0).astype(BF16)], axis=1))
    cq = [cq_ref[0, hh] for hh in range(2)]

    def stage_a(hh, j, masked):
        k0 = pl.multiple_of(j * kt, kt)
        kk = jnp.concatenate([k_ref[0, pl.ds(k0, kt), :], kb_ref[0, 0, pl.ds(k0, kt), :]], axis=1)
        s = lax.dot_general(kk, qh[hh], NT_DIMS, preferred_element_type=F32)
        if masked:
            s = jnp.where(t_row >= k0 + lax.broadcasted_iota(jnp.int32, (kt, tq), 0), s, -jnp.inf)
        s_buf[hh] = s
        return jnp.max(s, axis=0, keepdims=True)

    def stage_b(hh, tmax, m_i):
        m_new = jnp.maximum(m_i, tmax + cq[hh])
        p_buf[hh] = jnp.exp2(s_buf[hh] - (m_new - cq[hh])).astype(BF16)
        return jnp.exp2(m_i - m_new), m_new

    def stage_c(hh, j, alpha, acc):
        k0 = pl.multiple_of(j * kt, kt)
        pv = jnp.dot(vt_ref[0, hh * vr:(hh + 1) * vr, pl.ds(k0, kt)], p_buf[hh], preferred_element_type=F32)
        return alpha * acc + pv

    def step(hh, i, state, run_a):
        tmax, alpha, m_i, acc = state
        acc = stage_c(hh, jnp.where(i == 1, n_full, jnp.maximum(i - 2, 0)), alpha, acc)
        alpha, m_i = stage_b(hh, tmax, m_i)
        if run_a:
            tmax = stage_a(hh, i, False)
        return tmax, alpha, m_i, acc

    states = []
    for hh in range(2):
        tmax = stage_a(hh, n_full, True)
        p_buf[hh] = jnp.zeros((kt, tq), BF16)
        states.append((tmax, jnp.ones((1, tq), F32), jnp.full((1, tq), NEG_INF, F32), jnp.zeros((vr, tq), F32)))
    states = lax.fori_loop(0, n_full, lambda i, st: tuple(step(hh, i, st[hh], True) for hh in range(2)),
                           tuple(states))
    states = tuple(step(hh, n_full, states[hh], False) for hh in range(2))
    outs = []
    for hh in range(2):
        _, alpha, _, acc = states[hh]
        acc = stage_c(hh, jnp.maximum(n_full - 1, 0), alpha, acc)
        outs.append(acc[:HEAD_DIM] / acc[HEAD_DIM:HEAD_DIM + 1])
    o_ref[0] = jnp.concatenate(outs, axis=0).T.astype(o_ref.dtype)


def _fox_attention(fq, fk, kb, fvt, c_rows):
    b, s, _ = fq.shape
    n_pair = FOX_HEADS // 2
    assert FOX_KT % FOX_TQ == 0 and s % FOX_KT == 0
    return pl.pallas_call(
        _fox_kernel,
        grid=(b, n_pair, s // FOX_TQ),
        in_specs=[pl.BlockSpec((1, FOX_TQ, LANES), lambda bi, p, i: (bi, i, p)),
                  pl.BlockSpec((1, s, LANES), lambda bi, p, i: (bi, 0, p)),
                  pl.BlockSpec((1, 1, s, LANES), lambda bi, p, i: (bi, p, 0, 0)),
                  pl.BlockSpec((1, 2 * VROWS, s), lambda bi, p, i: (bi, p, 0)),
                  pl.BlockSpec((1, 2, 1, FOX_TQ), lambda bi, p, i: (bi, p, 0, i))],
        out_specs=pl.BlockSpec((1, FOX_TQ, LANES), lambda bi, p, i: (bi, i, p)),
        out_shape=jax.ShapeDtypeStruct((b, s, FOX_WIDTH), BF16),
        scratch_shapes=[pltpu.VMEM((2, FOX_KT, FOX_TQ), F32), pltpu.VMEM((2, FOX_KT, FOX_TQ), BF16)],
        compiler_params=_cparams("parallel", "parallel", "arbitrary"),
        name="fox_attention",
    )(fq, fk, kb, fvt, c_rows)


CONV_TS = 512
CONV_HALO = 32


def _conv_kernel(u_ref, uh_ref, w_ref, b_ref, g_ref, be_ref, o_ref, y_ref, sh_ref, *, ksize):
    ts, halo, cw = CONV_TS, CONV_HALO, CONV_WIDTH

    def glu(u):
        return u[:, :cw] * jax.nn.sigmoid(u[:, cw:])

    first = pl.program_id(1) == 0
    y_ref[0:halo, :] = jnp.where(first, 0.0, glu(uh_ref[0]))
    y_ref[halo:halo + ts, :] = glu(u_ref[0])
    span = ts + halo - 8
    for off in range(1, 8):
        sh_ref[off - 1, 0:span, :] = y_ref[off:off + span, :]
    acc = jnp.zeros((ts, cw), F32)
    for k in range(ksize):
        start = halo - (ksize - 1) + k
        a, off = divmod(start, 8)
        tap = y_ref[start:start + ts, :] if off == 0 else sh_ref[off - 1, 8 * a:8 * a + ts, :]
        acc = acc + w_ref[k:k + 1, :] * tap
    hn = _ln_rows(acc + b_ref[...], g_ref[...], be_ref[...])
    o_ref[0] = jax.nn.silu(hn).astype(o_ref.dtype)


def _conv_mixer(u, w_dw, b_dw, ln_g, ln_b):
    b, s, _ = u.shape
    ksize = w_dw.shape[0]
    assert ksize - 1 <= CONV_HALO
    ratio = CONV_TS // CONV_HALO
    vec = lambda a: a.reshape(1, CONV_WIDTH)
    const = lambda shape: pl.BlockSpec(shape, lambda bi, i: (0, 0))
    return pl.pallas_call(
        functools.partial(_conv_kernel, ksize=ksize),
        grid=(b, s // CONV_TS),
        in_specs=[pl.BlockSpec((1, CONV_TS, 2 * CONV_WIDTH), lambda bi, i: (bi, i, 0)),
                  pl.BlockSpec((1, CONV_HALO, 2 * CONV_WIDTH), lambda bi, i: (bi, jnp.maximum(i * ratio - 1, 0), 0)),
                  const((ksize, CONV_WIDTH)), const((1, CONV_WIDTH)), const((1, CONV_WIDTH)),
                  const((1, CONV_WIDTH))],
        out_specs=pl.BlockSpec((1, CONV_TS, CONV_WIDTH), lambda bi, i: (bi, i, 0)),
        out_shape=jax.ShapeDtypeStruct((b, s, CONV_WIDTH), BF16),
        scratch_shapes=[pltpu.VMEM((CONV_HALO + CONV_TS, CONV_WIDTH), F32),
                        pltpu.VMEM((7, CONV_HALO + CONV_TS - 8, CONV_WIDTH), F32)],
        compiler_params=_cparams("parallel", "arbitrary"),
        name="conv_mixer",
    )(u, u, w_dw, vec(b_dw), vec(ln_g), vec(ln_b))


def _outproj_kernel(oa_ref, ob_ref, oc_ref, x_ref, wa_ref, wb_ref, wc_ref, g_ref, b_ref, o_ref, *, alpha):
    mix = jnp.dot(oa_ref[...], wa_ref[...], preferred_element_type=F32)
    mix = mix + jnp.dot(ob_ref[...], wb_ref[...], preferred_element_type=F32)
    mix = mix + jnp.dot(oc_ref[...], wc_ref[...], preferred_element_type=F32)
    o_ref[...] = _ln_rows(alpha * x_ref[...] + mix, g_ref[...], b_ref[...])


def _out_proj(o_a, o_b, o_c, x2d, w_out, g, b, alpha, tm=512):
    t, d = x2d.shape
    wa = w_out[:NSA_WIDTH].astype(BF16)
    wb = w_out[NSA_WIDTH:NSA_WIDTH + CONV_WIDTH].astype(BF16)
    wc = w_out[NSA_WIDTH + CONV_WIDTH:].astype(BF16)
    rows = lambda w: pl.BlockSpec((tm, w), lambda i: (i, 0))
    full = lambda a: pl.BlockSpec(a.shape, lambda i: (0, 0))
    return pl.pallas_call(
        functools.partial(_outproj_kernel, alpha=alpha),
        grid=(t // tm,),
        in_specs=[rows(NSA_WIDTH), rows(CONV_WIDTH), rows(FOX_WIDTH), rows(d), full(wa), full(wb), full(wc),
                  pl.BlockSpec((1, d), lambda i: (0, 0)), pl.BlockSpec((1, d), lambda i: (0, 0))],
        out_specs=rows(d),
        out_shape=jax.ShapeDtypeStruct((t, d), F32),
        compiler_params=_cparams("parallel"),
        name="out_proj_ln",
    )(o_a, o_b, o_c, x2d, wa, wb, wc, g.reshape(1, d), b.reshape(1, d))


FFN_TM = 1024
FFN_HALO = 8
FFN_FC = 256


def _ffn_kernel(x_ref, xh_ref, wg_ref, wu_ref, wd_ref, cw_ref, cb_ref, g_ref, b_ref, o_ref, gate_ref, h_ref,
                *, alpha, n_chunks, ksize):
    tm, halo, fc = FFN_TM, FFN_HALO, FFN_FC
    x = x_ref[0]
    xb = x.astype(BF16)
    xhb = xh_ref[0].astype(BF16)
    first = pl.program_id(1) == 0
    for c in range(n_chunks):
        cs = slice(c * fc, (c + 1) * fc)
        gate_ref[0:halo, :] = jnp.where(first, 0.0, jnp.dot(xhb, wg_ref[:, cs], preferred_element_type=F32))
        gate_ref[halo:halo + tm, :] = jnp.dot(xb, wg_ref[:, cs], preferred_element_type=F32)
        conv = cb_ref[:, cs]
        for k in range(ksize):
            start = halo - (ksize - 1) + k
            conv = conv + cw_ref[k:k + 1, cs] * gate_ref[start:start + tm, :]
        up = jnp.dot(xb, wu_ref[:, cs], preferred_element_type=F32)
        h_ref[:, cs] = (jax.nn.silu(conv) * up).astype(BF16)
    down = jnp.dot(h_ref[...], wd_ref[...], preferred_element_type=F32)
    o_ref[0] = _ln_rows(alpha * x + down, g_ref[...], b_ref[...])


def _conv_ffn(x3d, w_ffn_in, w_conv, b_conv, w_down, g, b, alpha):
    bsz, s, d = x3d.shape
    d_ff = w_down.shape[0]
    ksize = w_conv.shape[0]
    assert d_ff % FFN_FC == 0 and ksize - 1 <= FFN_HALO
    wg = w_ffn_in[:, :d_ff].astype(BF16)
    wu = w_ffn_in[:, d_ff:].astype(BF16)
    wd = w_down.astype(BF16)
    ratio = FFN_TM // FFN_HALO
    const = lambda shape: pl.BlockSpec(shape, lambda bi, i: (0, 0))
    return pl.pallas_call(
        functools.partial(_ffn_kernel, alpha=alpha, n_chunks=d_ff // FFN_FC, ksize=ksize),
        grid=(bsz, s // FFN_TM),
        in_specs=[pl.BlockSpec((1, FFN_TM, d), lambda bi, i: (bi, i, 0)),
                  pl.BlockSpec((1, FFN_HALO, d), lambda bi, i: (bi, jnp.maximum(i * ratio - 1, 0), 0)),
                  const((d, d_ff)), const((d, d_ff)), const((d_ff, d)),
                  const((ksize, d_ff)), const((1, d_ff)), const((1, d)), const((1, d))],
        out_specs=pl.BlockSpec((1, FFN_TM, d), lambda bi, i: (bi, i, 0)),
        out_shape=jax.ShapeDtypeStruct((bsz, s, d), F32),
        scratch_shapes=[pltpu.VMEM((FFN_HALO + FFN_TM, FFN_FC), F32), pltpu.VMEM((FFN_TM, d_ff), BF16)],
        compiler_params=_cparams("parallel", "arbitrary"),
        name="conv_ffn_ln",
    )(x3d, x3d, wg, wu, wd, w_conv, b_conv.reshape(1, d_ff), g.reshape(1, d), b.reshape(1, d))


def kernel(x, ln_emb_g, ln_emb_b, w_in, b_f, w_cmp1, w_cmp2, pe_cmp, w_dw, b_dw, ln_conv_g, ln_conv_b, w_out,
           ln1_g, ln1_b, w_ffn_in, w_ffn_conv, b_ffn_conv, w_ffn_down, ln2_g, ln2_b):
    bsz, s, d = x.shape
    depth = w_in.shape[0]
    alpha = (2.0 * depth) ** 0.25
    t = bsz * s

    xc = x.reshape(t, d)
    for l in range(depth):
        proj = _project(xc.reshape(bsz, s, d), ln_emb_g, ln_emb_b, *_split_w_in(w_in[l]), apply_ln=(l == 0))
        (nq, kslc, kwin, fq, fk, cmp_k, cmp_v, conv_u, vslct, vwint, fvt, gates_t, f_t) = proj[:13]
        if l == 0:
            xc = proj[13].reshape(t, d)

        kc, vct = _compress(cmp_k, cmp_v, pe_cmp[l], w_cmp1[l], w_cmp2[l])
        o_a = _nsa_attention(nq, kc, vct, kslc, vslct, kwin, vwint, gates_t)

        o_b = _conv_mixer(conv_u, w_dw[l], b_dw[l], ln_conv_g[l], ln_conv_b[l])

        c, pieces = _fox_cumsum(f_t, b_f[l])
        c_rows = c[:, :FOX_HEADS].reshape(bsz, FOX_HEADS, 1, s)
        kb = pieces[:, :, :FOX_HEADS].reshape(bsz, BIAS_PIECES, FOX_HEADS // 2, 2, s)
        kb = kb.transpose(0, 2, 4, 3, 1).reshape(bsz, FOX_HEADS // 2, s, 2 * BIAS_PIECES)
        kb = jnp.pad(kb, ((0, 0), (0, 0), (0, 0), (0, LANES - 2 * BIAS_PIECES))).astype(BF16)
        o_c = _fox_attention(fq, fk, kb, fvt, c_rows)

        xc = _out_proj(o_a.reshape(t, NSA_WIDTH), o_b.reshape(t, CONV_WIDTH), o_c.reshape(t, FOX_WIDTH),
                       xc, w_out[l], ln1_g[l], ln1_b[l], alpha)
        xc = _conv_ffn(xc.reshape(bsz, s, d), w_ffn_in[l], w_ffn_conv[l], b_ffn_conv[l], w_ffn_down[l],
                       ln2_g[l], ln2_b[l], alpha).reshape(t, d)
    return xc.reshape(bsz, s, d)
```

```python
import functools

import jax
import jax.numpy as jnp
import numpy as np
from jax import lax
from jax.experimental import pallas as pl
from jax.experimental.pallas import tpu as pltpu

HEAD_DIM = 64
NSA_HEADS = 6
NSA_KV_HEADS = 2
NSA_GROUP = 3
FOX_HEADS = 6
NSA_WIDTH = NSA_HEADS * HEAD_DIM
CONV_WIDTH = 256
FOX_WIDTH = FOX_HEADS * HEAD_DIM
CMP_BLOCK = 32
SEL_BLOCK = 64
SEL_TOPK = 16
WINDOW = 512
LN_EPS = 1e-5
NEG_INF = -1e30
SCALE = HEAD_DIM ** -0.5
LOG2E = 1.4426950408889634
BIAS_PIECES = 3
BIAS_LANE0 = HEAD_DIM
PAD_LANE = BIAS_LANE0 + 2 * BIAS_PIECES
VROWS = HEAD_DIM + 16
MASK_BIG = 2.0 ** 100

LANES = 128
VMEM_LIMIT = 56 * 1024 * 1024

BF16 = jnp.bfloat16
F32 = jnp.float32
NT_DIMS = (((1,), (1,)), ((), ()))


def _cparams(*sem):
    return pltpu.CompilerParams(dimension_semantics=sem, vmem_limit_bytes=VMEM_LIMIT)


def _ln_rows(x, g, b):
    mu = jnp.mean(x, axis=-1, keepdims=True)
    xc = x - mu
    var = jnp.mean(xc * xc, axis=-1, keepdims=True)
    return xc * lax.rsqrt(var + LN_EPS) * g + b


def _bf16_pieces(value, n=BIAS_PIECES):
    out = []
    rest = np.float32(value)
    for _ in range(n):
        bits = np.float32(rest).view(np.uint32)
        bits = (bits + np.uint32(0x7FFF) + ((bits >> np.uint32(16)) & np.uint32(1))) & np.uint32(0xFFFF0000)
        piece = bits.view(np.float32)
        out.append(float(piece))
        rest = np.float32(rest - piece)
    return out


def _position_lanes(pos):
    hi = ((pos // SEL_BLOCK) * SEL_BLOCK).astype(F32)
    lo = (pos % SEL_BLOCK).astype(F32)
    lanes = jnp.zeros((pos.shape[0], LANES), F32)
    for i in range(BIAS_PIECES):
        lanes = lanes.at[:, BIAS_LANE0 + 2 * i].set(hi)
        lanes = lanes.at[:, BIAS_LANE0 + 2 * i + 1].set(lo)
    return lanes


NSA_Q_COLS = NSA_WIDTH
NSA_K_COLS = NSA_KV_HEADS * LANES
ROWS_SPLIT = (NSA_Q_COLS, NSA_K_COLS, NSA_K_COLS, FOX_WIDTH, FOX_WIDTH)
F32_SPLIT = (128, 128, 2 * CONV_WIDTH)
T_SPLIT = (128, 128, FOX_WIDTH, 24, 8)
T_ROWS = tuple(w // HEAD_DIM * VROWS for w in T_SPLIT[:3]) + T_SPLIT[3:]


def _proj_kernel(x_ref, g_ref, b_ref, wr_ref, wf_ref, wt_ref, pos_ref,
                 nq_ref, kslc_ref, kwin_ref, fq_ref, fk_ref, cmpk_ref, cmpv_ref, conv_ref,
                 vslc_ref, vwin_ref, fv_ref, gate_ref, flog_ref, *xn_ref):
    x = x_ref[0]
    if xn_ref:
        x = _ln_rows(x, g_ref[...], b_ref[...])
        xn_ref[0][0] = x
    xb = x.astype(BF16)
    rows = jnp.dot(xb, wr_ref[...], preferred_element_type=F32)
    off = 0
    for ref, w in zip((nq_ref, kslc_ref, kwin_ref, fq_ref, fk_ref), ROWS_SPLIT):
        blk = rows[:, off:off + w]
        if ref is kslc_ref or ref is kwin_ref:
            blk = blk + pos_ref[...]
        ref[0] = blk.astype(ref.dtype)
        off += w
    f32o = jnp.dot(xb, wf_ref[...], preferred_element_type=F32)
    off = 0
    for ref, w in zip((cmpk_ref, cmpv_ref, conv_ref), F32_SPLIT):
        ref[0] = f32o[:, off:off + w]
        off += w
    tr = lax.dot_general(wt_ref[...], xb, NT_DIMS, preferred_element_type=F32)
    ones = jnp.ones((VROWS - HEAD_DIM, tr.shape[1]), F32)
    off = 0
    for ref, w in zip((vslc_ref, vwin_ref, fv_ref, gate_ref, flog_ref), T_SPLIT):
        blk = tr[off:off + w, :]
        if ref.shape[1] != w:
            blk = jnp.concatenate([part for h in range(w // HEAD_DIM)
                                   for part in (blk[h * HEAD_DIM:(h + 1) * HEAD_DIM], ones)], axis=0)
        ref[0] = blk.astype(ref.dtype)
        off += w


def _split_w_in(w_in_l):
    kvw = NSA_KV_HEADS * HEAD_DIM
    o = 0
    w_nq = w_in_l[:, o:o + NSA_WIDTH]; o += NSA_WIDTH
    w_kv = w_in_l[:, o:o + 6 * kvw]; o += 6 * kvw
    w_g = w_in_l[:, o:o + 3 * NSA_HEADS]; o += 3 * NSA_HEADS
    w_conv = w_in_l[:, o:o + 2 * CONV_WIDTH]; o += 2 * CONV_WIDTH
    w_fqkv = w_in_l[:, o:o + 3 * FOX_WIDTH]; o += 3 * FOX_WIDTH
    w_f = w_in_l[:, o:o + FOX_HEADS]
    d = w_in_l.shape[0]

    def head_blocks(w, n_heads):
        w = w.reshape(d, n_heads, HEAD_DIM)
        return jnp.pad(w, ((0, 0), (0, 0), (0, LANES - HEAD_DIM))).reshape(d, n_heads * LANES)

    w_nq = w_nq * (SCALE * LOG2E)
    kv = [w_kv[:, i * kvw:(i + 1) * kvw] for i in range(6)]
    w_fq = w_fqkv[:, :FOX_WIDTH] * (SCALE * LOG2E)
    w_fk = w_fqkv[:, FOX_WIDTH:2 * FOX_WIDTH]
    w_fv = w_fqkv[:, 2 * FOX_WIDTH:]
    w_rows = jnp.concatenate([w_nq, head_blocks(kv[2], NSA_KV_HEADS), head_blocks(kv[4], NSA_KV_HEADS),
                              w_fq, w_fk], axis=1).astype(BF16)
    w_f32 = jnp.concatenate([kv[0], kv[1], w_conv], axis=1).astype(BF16)
    w_gp = jnp.pad(w_g, ((0, 0), (0, T_SPLIT[3] - w_g.shape[1])))
    w_fp = jnp.pad(w_f, ((0, 0), (0, T_SPLIT[4] - w_f.shape[1])))
    w_t = jnp.concatenate([kv[3], kv[5], w_fv, w_gp, w_fp], axis=1).T.astype(BF16)
    return w_rows, w_f32, w_t


def _project(x3d, ln_g, ln_b, w_rows, w_f32, w_t, apply_ln, tm=512):
    b, s, d = x3d.shape
    pos_lanes = jnp.tile(_position_lanes(jnp.arange(s)), (1, NSA_KV_HEADS))
    row_spec = lambda w: pl.BlockSpec((1, tm, w), lambda bi, i: (bi, i, 0))
    t_spec = lambda w: pl.BlockSpec((1, w, tm), lambda bi, i: (bi, 0, i))
    full = lambda a: pl.BlockSpec(a.shape, lambda bi, i: (0, 0))
    out_shape = ([jax.ShapeDtypeStruct((b, s, w), BF16) for w in ROWS_SPLIT]
                 + [jax.ShapeDtypeStruct((b, s, w), F32) for w in F32_SPLIT]
                 + [jax.ShapeDtypeStruct((b, w, s), dt) for w, dt in zip(T_ROWS, (BF16, BF16, BF16, F32, F32))])
    out_specs = [row_spec(w) for w in ROWS_SPLIT + F32_SPLIT] + [t_spec(w) for w in T_ROWS]
    if apply_ln:
        out_shape.append(jax.ShapeDtypeStruct((b, s, d), F32))
        out_specs.append(row_spec(d))
    vec = pl.BlockSpec((1, d), lambda bi, i: (0, 0))
    return pl.pallas_call(
        _proj_kernel,
        grid=(b, s // tm),
        in_specs=[row_spec(d), vec, vec, full(w_rows), full(w_f32), full(w_t),
                  pl.BlockSpec((tm, NSA_K_COLS), lambda bi, i: (i, 0))],
        out_specs=out_specs,
        out_shape=out_shape,
        compiler_params=_cparams("parallel", "parallel"),
        name="in_proj",
    )(x3d, ln_g.reshape(1, d), ln_b.reshape(1, d), w_rows, w_f32, w_t, pos_lanes)


def _compress_kernel(rawk_ref, rawv_ref, pe_ref, w1_ref, w2k_ref, w2vt_ref, end_ref, kc_ref, vct_ref):
    s = rawk_ref.shape[1]
    half = s // (2 * CMP_BLOCK)
    hidden = w1_ref.shape[-1] // NSA_KV_HEADS
    outs = []
    for i, raw_ref in enumerate((rawk_ref, rawv_ref)):
        hid = jnp.zeros((2 * half, w1_ref.shape[-1]), F32)
        for p in range(CMP_BLOCK):
            rows = jnp.concatenate(
                [raw_ref[0, pl.ds(par * CMP_BLOCK + p, half, stride=2 * CMP_BLOCK), :]
                 for par in range(2)], axis=0)
            hid = hid + jnp.dot((rows + pe_ref[i, p:p + 1, :]).astype(BF16), w1_ref[i, p],
                                preferred_element_type=F32)
        outs.append(jax.nn.gelu(hid).astype(BF16))
    kc_ref[0] = jnp.concatenate(
        [jnp.dot(outs[0][:, g * hidden:(g + 1) * hidden], w2k_ref[...], preferred_element_type=F32) + end_ref[...]
         for g in range(NSA_KV_HEADS)], axis=1).astype(kc_ref.dtype)
    vct_ref[0] = jnp.concatenate(
        [lax.dot_general(w2vt_ref[...], outs[1][:, g * hidden:(g + 1) * hidden], NT_DIMS,
                         preferred_element_type=F32) for g in range(NSA_KV_HEADS)], axis=0).astype(vct_ref.dtype)


def _compress(cmp_k, cmp_v, pe, w1, w2):
    b, s, _ = cmp_k.shape
    n_cmp = s // CMP_BLOCK
    hidden = w1.shape[-1]
    pe2 = jnp.tile(pe, (1, 1, NSA_KV_HEADS))
    w1p = w1.reshape(2, CMP_BLOCK, HEAD_DIM, hidden)
    zero = jnp.zeros_like(w1p)
    w1blk = jnp.concatenate([jnp.concatenate([w1p, zero], axis=-1),
                             jnp.concatenate([zero, w1p], axis=-1)], axis=2).astype(BF16)
    w2k = jnp.pad(w2[0], ((0, 0), (0, LANES - HEAD_DIM))).astype(BF16)
    w2vt = w2[1].T.astype(BF16)
    blk = jnp.concatenate([jnp.arange(0, n_cmp, 2), jnp.arange(1, n_cmp, 2)])
    end_lanes = _position_lanes(blk * CMP_BLOCK + CMP_BLOCK - 1)
    const = lambda a: pl.BlockSpec(a.shape, lambda i: (0,) * a.ndim)
    return pl.pallas_call(
        _compress_kernel,
        grid=(b,),
        in_specs=[pl.BlockSpec((1, s, LANES), lambda i: (i, 0, 0)), pl.BlockSpec((1, s, LANES), lambda i: (i, 0, 0)),
                  const(pe2), const(w1blk), const(w2k), const(w2vt), const(end_lanes)],
        out_specs=[pl.BlockSpec((1, n_cmp, NSA_K_COLS), lambda i: (i, 0, 0)),
                   pl.BlockSpec((1, NSA_KV_HEADS * HEAD_DIM, n_cmp), lambda i: (i, 0, 0))],
        out_shape=[jax.ShapeDtypeStruct((b, n_cmp, NSA_K_COLS), BF16),
                   jax.ShapeDtypeStruct((b, NSA_KV_HEADS * HEAD_DIM, n_cmp), BF16)],
        compiler_params=_cparams("parallel"),
        name="nsa_compress",
    )(cmp_k, cmp_v, pe2, w1blk, w2k, w2vt, end_lanes)


NSA_TQ = 256
NSA_KT = 512


def _alibi_slope(h):
    return float(2.0 ** (-8.0 * (h + 1) / NSA_HEADS))


def _nsa_kernel(q_ref, kc_ref, vct_ref, kslc_ref, vslct_ref, kwin_ref, vwint_ref, gate_ref, e_ref, o_ref,
                s_buf, p_buf, idx_ref, *, seq_len):
    tq, kt = NSA_TQ, NSA_KT
    n_cmp = seq_len // CMP_BLOCK
    n_sel = seq_len // SEL_BLOCK
    n_tiles = seq_len // kt
    top_k = min(SEL_TOPK, n_sel)
    kw = WINDOW + tq
    wide = NSA_GROUP * tq
    qi = pl.program_id(1)
    q0 = qi * tq
    t_row = q0 + lax.broadcasted_iota(jnp.int32, (1, tq), 1)
    t_f = t_row.astype(F32)
    lane = lax.broadcasted_iota(jnp.int32, (1, LANES), 1)
    cur = t_row // SEL_BLOCK
    groups = range(NSA_KV_HEADS)

    def tile3(a):
        return jnp.concatenate([a] * NSA_GROUP, axis=1)

    q1, cq = [], []
    for g in groups:
        rows, consts = [], []
        for j in range(NSA_GROUP):
            h = g * NSA_GROUP + j
            slope2 = _alibi_slope(h) * LOG2E
            sl = jnp.where(lane == PAD_LANE, -MASK_BIG, jnp.zeros((1, LANES), F32))
            for i, piece in enumerate(_bf16_pieces(slope2)):
                sl = jnp.where((lane == BIAS_LANE0 + 2 * i) | (lane == BIAS_LANE0 + 2 * i + 1), piece, sl)
            qv = q_ref[0, :, (h // 2) * LANES:(h // 2 + 1) * LANES].astype(F32)
            if h % 2:
                qv = pltpu.roll(qv, HEAD_DIM, axis=1)
            rows.append((jnp.where(lane < HEAD_DIM, qv, 0.0) + sl).astype(BF16))
            consts.append(-slope2 * t_f)
        q1.append(jnp.concatenate(rows, axis=0))
        cq.append(jnp.concatenate(consts, axis=1))

    def softmax_weights(s, valid):
        ps = []
        for j in range(NSA_GROUP):
            sj = jnp.where(valid, s[:, j * tq:(j + 1) * tq], NEG_INF)
            m = jnp.max(sj, axis=0, keepdims=True)
            p = jnp.exp2(sj - m)
            l = jnp.sum(p, axis=0, keepdims=True)
            ps.append(p * jnp.where(m > 0.5 * NEG_INF, 1.0 / l, 0.0))
        return jnp.concatenate(ps, axis=1)

    def softmax_out(s, vt):
        m = jnp.max(s, axis=0, keepdims=True)
        pv = jnp.dot(vt, jnp.exp2(s - m).astype(BF16), preferred_element_type=F32)
        return pv[:HEAD_DIM] * (1.0 / pv[HEAD_DIM:HEAD_DIM + 1])

    def vrows(g):
        return slice(g * VROWS, (g + 1) * VROWS)

    def kcols(g):
        return slice(g * LANES, (g + 1) * LANES)

    r = lax.broadcasted_iota(jnp.int32, (n_cmp, tq), 0)
    blk_id = jnp.where(r < n_sel, 2 * r, 2 * (r - n_sel) + 1)
    valid_c = t_row - (blk_id * CMP_BLOCK + CMP_BLOCK - 1) >= 0
    o_c, score = [], []
    jid = lax.broadcasted_iota(jnp.int32, (n_sel, tq), 0)
    forced = (jid == 0) | (jid == cur)
    for g in groups:
        s_c = lax.dot_general(kc_ref[0, :, kcols(g)], q1[g], NT_DIMS, preferred_element_type=F32)
        p_c = softmax_weights(s_c, valid_c)
        o_c.append(jnp.dot(vct_ref[0, g * HEAD_DIM:(g + 1) * HEAD_DIM, :], p_c.astype(BF16),
                           preferred_element_type=F32))
        imp = p_c[:, 0:tq] + p_c[:, tq:2 * tq] + p_c[:, 2 * tq:3 * tq]
        imp = imp[0:n_sel, :] + imp[n_sel:2 * n_sel, :]
        score.append(jnp.where(jid > cur, -1.0, jnp.where(forced, NSA_GROUP + 1.0, imp)))

    for _ in range(top_k):
        for g in groups:
            mx = jnp.max(score[g], axis=0, keepdims=True)
            first = jnp.min(jnp.where(score[g] == mx, jid, n_sel), axis=0, keepdims=True)
            score[g] = jnp.where(jid == first, -2.0, score[g])
    sel = [jnp.where(score[g] == -2.0, 1.0, 0.0) for g in groups]

    w0 = pl.multiple_of(q0, tq)
    row = lax.broadcasted_iota(jnp.int32, (tq, tq), 0)
    col = lax.broadcasted_iota(jnp.int32, (tq, tq), 1)
    near = tile3(col < row)
    causal = tile3(row <= col)
    o_w = []
    for g in groups:
        s_w = lax.dot_general(kwin_ref[0, pl.ds(w0, kw), kcols(g)], q1[g], NT_DIMS, preferred_element_type=F32)
        s_w = jnp.concatenate([jnp.where(near, s_w[:tq], -jnp.inf), s_w[tq:WINDOW],
                               jnp.where(causal, s_w[WINDOW:], -jnp.inf)], axis=0)
        o_w.append(softmax_out(s_w, vwint_ref[0, vrows(g), pl.ds(w0, kw)]))

    kpos = q0 + lax.broadcasted_iota(jnp.int32, (tq, tq), 0)
    valid_o = tile3((kpos // SEL_BLOCK == cur) & (kpos <= t_row))
    states, q2, counts = [], [], []
    past = jid < cur
    tile_of_blk = (lax.broadcasted_iota(jnp.int32, (max(n_tiles, 8), n_sel), 1) // (kt // SEL_BLOCK)
                   == lax.broadcasted_iota(jnp.int32, (max(n_tiles, 8), n_sel), 0)).astype(BF16)
    for g in groups:
        s = lax.dot_general(kslc_ref[0, pl.ds(q0, tq), kcols(g)], q1[g], NT_DIMS, preferred_element_type=F32)
        s = jnp.where(valid_o, s, -jnp.inf)
        m0 = jnp.max(s, axis=0, keepdims=True) + cq[g]
        p = jnp.exp2(s - (m0 - cq[g]))
        acc0 = jnp.dot(vslct_ref[0, vrows(g), pl.ds(q0, tq)], p.astype(BF16), preferred_element_type=F32)
        p_buf[g] = jnp.zeros((kt, wide), BF16)
        states.append((jnp.ones((1, wide), F32), m0, acc0))

        sel_sw = jnp.where(past, sel[g], 0.0)
        sel_pad = jnp.concatenate([sel_sw, jnp.zeros((LANES - n_sel, tq), F32)], axis=0) if n_sel < LANES else sel_sw
        mask_lanes = ((sel_pad.T - 1.0) * MASK_BIG).astype(BF16)
        q2.append(jnp.concatenate([q1[g], jnp.concatenate([mask_lanes] * NSA_GROUP, axis=0)], axis=1))
        per_tile = jnp.dot(tile_of_blk, sel_sw.astype(BF16), preferred_element_type=F32)
        counts.append(jnp.dot(per_tile.astype(BF16), jnp.ones((tq, LANES), BF16), preferred_element_type=F32))

    n_act = []
    for g in groups:
        for t in range(n_tiles + 1):
            idx_ref[g, t] = n_tiles
        c = jnp.int32(0)
        for t in range(n_tiles):
            idx_ref[g, c] = t
            c = c + (counts[g][t, 0] > 0.5).astype(jnp.int32)
        idx_ref[g, c] = n_tiles
        n_act.append(c)
    n_trips = jnp.maximum(n_act[0], n_act[1])

    def key_start(tile):
        return pl.multiple_of(jnp.minimum(tile, n_tiles - 1) * kt, kt)

    def stage_a(g, tile):
        e0 = pl.multiple_of(tile * kt, kt)
        kk = jnp.concatenate([kslc_ref[0, pl.ds(key_start(tile), kt), kcols(g)], e_ref[pl.ds(e0, kt), :]],
                             axis=1)
        s = lax.dot_general(kk, q2[g], NT_DIMS, preferred_element_type=F32)
        s_buf[g] = s
        return jnp.max(s, axis=0, keepdims=True)

    def stage_b(g, tmax, m_i):
        m_new = jnp.maximum(m_i, tmax + cq[g])
        p_buf[g] = jnp.exp2(s_buf[g] - (m_new - cq[g])).astype(BF16)
        return jnp.exp2(m_i - m_new), m_new

    def stage_c(g, tile, alpha, acc):
        pv = jnp.dot(vslct_ref[0, vrows(g), pl.ds(key_start(tile), kt)], p_buf[g], preferred_element_type=F32)
        return alpha * acc + pv

    def step(g, i, state, run_a):
        tmax, alpha, m_i, acc = state
        acc = stage_c(g, idx_ref[g, jnp.maximum(i - 2, 0)], alpha, acc)
        alpha, m_i = stage_b(g, tmax, m_i)
        if run_a:
            tmax = stage_a(g, idx_ref[g, i])
        return tmax, alpha, m_i, acc

    n_trips = jnp.maximum(n_trips, 1)
    states = tuple((stage_a(g, idx_ref[g, 0]),) + states[g] for g in groups)
    states = lax.fori_loop(1, n_trips, lambda i, st: tuple(step(g, i, st[g], True) for g in groups), states)
    states = tuple(step(g, n_trips, states[g], False) for g in groups)

    out_rows = [None] * NSA_HEADS
    for g in groups:
        _, alpha, _, acc = states[g]
        acc = stage_c(g, idx_ref[g, n_trips - 1], alpha, acc)
        o_s = acc[:HEAD_DIM] * (1.0 / acc[HEAD_DIM:HEAD_DIM + 1])
        for j in range(NSA_GROUP):
            h = g * NSA_GROUP + j
            gt = jax.nn.sigmoid(gate_ref[0, 3 * h:3 * h + 3, :])
            sl = slice(j * tq, (j + 1) * tq)
            out_rows[h] = gt[0:1] * o_c[g][:, sl] + gt[1:2] * o_s[:, sl] + gt[2:3] * o_w[g][:, sl]
    o_ref[0] = jnp.concatenate(out_rows, axis=0).T.astype(o_ref.dtype)


def _nsa_attention(nq, kc, vct, kslc, vslct, kwin, vwint, gates_t):
    b, s, _ = nq.shape
    n_cmp = s // CMP_BLOCK
    n_sel = s // SEL_BLOCK
    n_tiles = s // NSA_KT
    assert SEL_TOPK < n_sel <= LANES and s % NSA_KT == 0 and NSA_TQ <= WINDOW
    e_sel = jnp.arange(s)[:, None] // SEL_BLOCK == jnp.arange(LANES)[None, :]
    e_sel = jnp.concatenate([e_sel, jnp.ones((NSA_KT, LANES), bool)], axis=0).astype(BF16)
    pad_rows = jnp.zeros((WINDOW, LANES), BF16).at[:, PAD_LANE].set(1)
    pad_rows = jnp.broadcast_to(jnp.tile(pad_rows, (1, NSA_KV_HEADS))[None], (b, WINDOW, NSA_K_COLS))
    kwin = jnp.concatenate([pad_rows, kwin], axis=1)
    vwint = jnp.pad(vwint, ((0, 0), (0, 0), (WINDOW, 0)))
    vr = NSA_KV_HEADS * VROWS
    per_b = lambda shape: pl.BlockSpec((1,) + shape, lambda bi, i: (bi, 0, 0))
    wide = NSA_GROUP * NSA_TQ
    return pl.pallas_call(
        functools.partial(_nsa_kernel, seq_len=s),
        grid=(b, s // NSA_TQ),
        in_specs=[pl.BlockSpec((1, NSA_TQ, NSA_Q_COLS), lambda bi, i: (bi, i, 0)),
                  per_b((n_cmp, NSA_K_COLS)), per_b((LANES, n_cmp)),
                  per_b((s, NSA_K_COLS)), per_b((vr, s)),
                  per_b((s + WINDOW, NSA_K_COLS)), per_b((vr, s + WINDOW)),
                  pl.BlockSpec((1, gates_t.shape[1], NSA_TQ), lambda bi, i: (bi, 0, i)),
                  pl.BlockSpec(e_sel.shape, lambda bi, i: (0, 0))],
        out_specs=pl.BlockSpec((1, NSA_TQ, NSA_WIDTH), lambda bi, i: (bi, i, 0)),
        out_shape=jax.ShapeDtypeStruct((b, s, NSA_WIDTH), BF16),
        scratch_shapes=[pltpu.VMEM((NSA_KV_HEADS, NSA_KT, wide), F32), pltpu.VMEM((NSA_KV_HEADS, NSA_KT, wide), BF16),
                        pltpu.SMEM((NSA_KV_HEADS, n_tiles + 1), jnp.int32)],
        compiler_params=_cparams("parallel", "arbitrary"),
        name="nsa_attention",
    )(nq, kc, vct, kslc, vslct, kwin, vwint, gates_t, e_sel)


def _cumsum_kernel(f_ref, b_ref, c_ref, piece_ref):
    rows = f_ref.shape[1]
    per_head = rows // 8
    hi = lax.Precision.HIGHEST
    ri = lax.broadcasted_iota(jnp.int32, (LANES, LANES), 0)
    ci = lax.broadcasted_iota(jnp.int32, (LANES, LANES), 1)
    upper = (ri <= ci).astype(F32)
    rr = lax.broadcasted_iota(jnp.int32, (rows, rows), 0)
    rc = lax.broadcasted_iota(jnp.int32, (rows, rows), 1)
    earlier = ((rc < rr) & (rc // per_head == rr // per_head)).astype(F32)
    lf = jax.nn.log_sigmoid(f_ref[0] + b_ref[...])
    within = jnp.dot(lf, upper, preferred_element_type=F32, precision=hi)
    total = jnp.dot(lf, jnp.ones((LANES, LANES), F32), preferred_element_type=F32, precision=hi)
    carry = jnp.dot(earlier, total, preferred_element_type=F32, precision=hi)
    c = (within + carry) * LOG2E
    c_ref[0] = c
    rest = -c
    for i in range(BIAS_PIECES):
        piece = rest.astype(BF16).astype(F32)
        piece_ref[0, i] = piece
        rest = rest - piece


def _fox_cumsum(f_t, b_f):
    b, r, s = f_t.shape
    assert r == 8
    rows = r * (s // LANES)
    bias = jnp.repeat(jnp.pad(b_f, (0, r - b_f.shape[0])), s // LANES)
    bias = jnp.broadcast_to(bias[:, None], (rows, LANES)).astype(F32)
    c, pieces = pl.pallas_call(
        _cumsum_kernel,
        grid=(b,),
        in_specs=[pl.BlockSpec((1, rows, LANES), lambda i: (i, 0, 0)),
                  pl.BlockSpec((rows, LANES), lambda i: (0, 0))],
        out_specs=[pl.BlockSpec((1, rows, LANES), lambda i: (i, 0, 0)),
                   pl.BlockSpec((1, BIAS_PIECES, rows, LANES), lambda i: (i, 0, 0, 0))],
        out_shape=[jax.ShapeDtypeStruct((b, rows, LANES), F32),
                   jax.ShapeDtypeStruct((b, BIAS_PIECES, rows, LANES), F32)],
        compiler_params=_cparams("parallel"),
        name="fox_cumsum",
    )(f_t.reshape(b, rows, LANES), bias)
    return c.reshape(b, r, s), pieces.reshape(b, BIAS_PIECES, r, s)


FOX_TQ = 1024
FOX_KT = 1024


def _fox_kernel(q_ref, k_ref, kb_ref, vt_ref, cq_ref, o_ref, s_buf, p_buf):
    tq, kt, vr = FOX_TQ, FOX_KT, VROWS
    q0 = pl.program_id(2) * tq
    n_full = q0 // kt
    lane = lax.broadcasted_iota(jnp.int32, (tq, LANES), 1)
    t_row = q0 + lax.broadcasted_iota(jnp.int32, (1, tq), 1)
    qh = []
    for hh in range(2):
        pick = (lane >= hh * BIAS_PIECES) & (lane < (hh + 1) * BIAS_PIECES)
        qh.append(jnp.concatenate([jnp.where(lane // HEAD_DIM == hh, q_ref[0], 0).astype(BF16),
                                   jnp.where(pick, 1.0, 0.0).astype(BF16)], axis=1))
    cq = [cq_ref[0, hh] for hh in range(2)]

    def stage_a(hh, j, masked):
        k0 = pl.multiple_of(j * kt, kt)
        kk = jnp.concatenate([k_ref[0, pl.ds(k0, kt), :], kb_ref[0, 0, pl.ds(k0, kt), :]], axis=1)
        s = lax.dot_general(kk, qh[hh], NT_DIMS, preferred_element_type=F32)
        if masked:
            s = jnp.where(t_row >= k0 + lax.broadcasted_iota(jnp.int32, (kt, tq), 0), s, -jnp.inf)
        s_buf[hh] = s
        return jnp.max(s, axis=0, keepdims=True)

    def stage_b(hh, tmax, m_i):
        m_new = jnp.maximum(m_i, tmax + cq[hh])
        p_buf[hh] = jnp.exp2(s_buf[hh] - (m_new - cq[hh])).astype(BF16)
        return jnp.exp2(m_i - m_new), m_new

    def stage_c(hh, j, alpha, acc):
        k0 = pl.multiple_of(j * kt, kt)
        pv = jnp.dot(vt_ref[0, hh * vr:(hh + 1) * vr, pl.ds(k0, kt)], p_buf[hh], preferred_element_type=F32)
        return alpha * acc + pv

    def step(hh, i, state, run_a):
        tmax, alpha, m_i, acc = state
        acc = stage_c(hh, jnp.where(i == 1, n_full, jnp.maximum(i - 2, 0)), alpha, acc)
        alpha, m_i = stage_b(hh, tmax, m_i)
        if run_a:
            tmax = stage_a(hh, i, False)
        return tmax, alpha, m_i, acc

    states = []
    for hh in range(2):
        tmax = stage_a(hh, n_full, True)
        p_buf[hh] = jnp.zeros((kt, tq), BF16)
        states.append((tmax, jnp.ones((1, tq), F32), jnp.full((1, tq), NEG_INF, F32), jnp.zeros((vr, tq), F32)))
    states = lax.fori_loop(0, n_full, lambda i, st: tuple(step(hh, i, st[hh], True) for hh in range(2)),
                           tuple(states))
    states = tuple(step(hh, n_full, states[hh], False) for hh in range(2))
    outs = []
    for hh in range(2):
        _, alpha, _, acc = states[hh]
        acc = stage_c(hh, jnp.maximum(n_full - 1, 0), alpha, acc)
        outs.append(acc[:HEAD_DIM] / acc[HEAD_DIM:HEAD_DIM + 1])
    o_ref[0] = jnp.concatenate(outs, axis=0).T.astype(o_ref.dtype)


def _fox_attention(fq, fk, kb, fvt, c_rows):
    b, s, _ = fq.shape
    n_pair = FOX_HEADS // 2
    assert FOX_KT % FOX_TQ == 0 and s % FOX_KT == 0
    return pl.pallas_call(
        _fox_kernel,
        grid=(b, n_pair, s // FOX_TQ),
        in_specs=[pl.BlockSpec((1, FOX_TQ, LANES), lambda bi, p, i: (bi, i, p)),
                  pl.BlockSpec((1, s, LANES), lambda bi, p, i: (bi, 0, p)),
                  pl.BlockSpec((1, 1, s, LANES), lambda bi, p, i: (bi, p, 0, 0)),
                  pl.BlockSpec((1, 2 * VROWS, s), lambda bi, p, i: (bi, p, 0)),
                  pl.BlockSpec((1, 2, 1, FOX_TQ), lambda bi, p, i: (bi, p, 0, i))],
        out_specs=pl.BlockSpec((1, FOX_TQ, LANES), lambda bi, p, i: (bi, i, p)),
        out_shape=jax.ShapeDtypeStruct((b, s, FOX_WIDTH), BF16),
        scratch_shapes=[pltpu.VMEM((2, FOX_KT, FOX_TQ), F32), pltpu.VMEM((2, FOX_KT, FOX_TQ), BF16)],
        compiler_params=_cparams("parallel", "parallel", "arbitrary"),
        name="fox_attention",
    )(fq, fk, kb, fvt, c_rows)


CONV_TS = 512
CONV_HALO = 32


def _conv_kernel(u_ref, uh_ref, w_ref, b_ref, g_ref, be_ref, o_ref, y_ref, sh_ref, *, ksize):
    ts, halo, cw = CONV_TS, CONV_HALO, CONV_WIDTH

    def glu(u):
        return u[:, :cw] * jax.nn.sigmoid(u[:, cw:])

    first = pl.program_id(1) == 0
    y_ref[0:halo, :] = jnp.where(first, 0.0, glu(uh_ref[0]))
    y_ref[halo:halo + ts, :] = glu(u_ref[0])
    span = ts + halo - 8
    for off in range(1, 8):
        sh_ref[off - 1, 0:span, :] = y_ref[off:off + span, :]
    acc = jnp.zeros((ts, cw), F32)
    for k in range(ksize):
        start = halo - (ksize - 1) + k
        a, off = divmod(start, 8)
        tap = y_ref[start:start + ts, :] if off == 0 else sh_ref[off - 1, 8 * a:8 * a + ts, :]
        acc = acc + w_ref[k:k + 1, :] * tap
    hn = _ln_rows(acc + b_ref[...], g_ref[...], be_ref[...])
    o_ref[0] = jax.nn.silu(hn).astype(o_ref.dtype)


def _conv_mixer(u, w_dw, b_dw, ln_g, ln_b):
    b, s, _ = u.shape
    ksize = w_dw.shape[0]
    assert ksize - 1 <= CONV_HALO
    ratio = CONV_TS // CONV_HALO
    vec = lambda a: a.reshape(1, CONV_WIDTH)
    const = lambda shape: pl.BlockSpec(shape, lambda bi, i: (0, 0))
    return pl.pallas_call(
        functools.partial(_conv_kernel, ksize=ksize),
        grid=(b, s // CONV_TS),
        in_specs=[pl.BlockSpec((1, CONV_TS, 2 * CONV_WIDTH), lambda bi, i: (bi, i, 0)),
                  pl.BlockSpec((1, CONV_HALO, 2 * CONV_WIDTH), lambda bi, i: (bi, jnp.maximum(i * ratio - 1, 0), 0)),
                  const((ksize, CONV_WIDTH)), const((1, CONV_WIDTH)), const((1, CONV_WIDTH)),
                  const((1, CONV_WIDTH))],
        out_specs=pl.BlockSpec((1, CONV_TS, CONV_WIDTH), lambda bi, i: (bi, i, 0)),
        out_shape=jax.ShapeDtypeStruct((b, s, CONV_WIDTH), BF16),
        scratch_shapes=[pltpu.VMEM((CONV_HALO + CONV_TS, CONV_WIDTH), F32),
                        pltpu.VMEM((7, CONV_HALO + CONV_TS - 8, CONV_WIDTH), F32)],
        compiler_params=_cparams("parallel", "arbitrary"),
        name="conv_mixer",
    )(u, u, w_dw, vec(b_dw), vec(ln_g), vec(ln_b))


def _outproj_kernel(oa_ref, ob_ref, oc_ref, x_ref, wa_ref, wb_ref, wc_ref, g_ref, b_ref, o_ref, *, alpha):
    mix = jnp.dot(oa_ref[...], wa_ref[...], preferred_element_type=F32)
    mix = mix + jnp.dot(ob_ref[...], wb_ref[...], preferred_element_type=F32)
    mix = mix + jnp.dot(oc_ref[...], wc_ref[...], preferred_element_type=F32)
    o_ref[...] = _ln_rows(alpha * x_ref[...] + mix, g_ref[...], b_ref[...])


def _out_proj(o_a, o_b, o_c, x2d, w_out, g, b, alpha, tm=512):
    t, d = x2d.shape
    wa = w_out[:NSA_WIDTH].astype(BF16)
    wb = w_out[NSA_WIDTH:NSA_WIDTH + CONV_WIDTH].astype(BF16)
    wc = w_out[NSA_WIDTH + CONV_WIDTH:].astype(BF16)
    rows = lambda w: pl.BlockSpec((tm, w), lambda i: (i, 0))
    full = lambda a: pl.BlockSpec(a.shape, lambda i: (0, 0))
    return pl.pallas_call(
        functools.partial(_outproj_kernel, alpha=alpha),
        grid=(t // tm,),
        in_specs=[rows(NSA_WIDTH), rows(CONV_WIDTH), rows(FOX_WIDTH), rows(d), full(wa), full(wb), full(wc),
                  pl.BlockSpec((1, d), lambda i: (0, 0)), pl.BlockSpec((1, d), lambda i: (0, 0))],
        out_specs=rows(d),
        out_shape=jax.ShapeDtypeStruct((t, d), F32),
        compiler_params=_cparams("parallel"),
        name="out_proj_ln",
    )(o_a, o_b, o_c, x2d, wa, wb, wc, g.reshape(1, d), b.reshape(1, d))


FFN_TM = 1024
FFN_HALO = 16
FFN_FC = 256


def _ffn_kernel(x_ref, xh_ref, wg_ref, wu_ref, wd_ref, cw_ref, cb_ref, g_ref, b_ref, o_ref, gate_ref, h_ref,
                *, alpha, n_chunks, ksize):
    tm, halo, fc = FFN_TM, FFN_HALO, FFN_FC
    x = x_ref[0]
    xb = x.astype(BF16)
    xeb = jnp.concatenate([xh_ref[0].astype(BF16), xb], axis=0)
    first = pl.program_id(1) == 0
    for c in range(n_chunks):
        cs = slice(c * fc, (c + 1) * fc)
        gate = jnp.dot(xeb, wg_ref[:, cs], preferred_element_type=F32)
        gate_ref[0:halo, :] = jnp.where(first, 0.0, gate[0:halo])
        gate_ref[halo:halo + tm, :] = gate[halo:]
        conv = cb_ref[:, cs]
        for k in range(ksize):
            start = halo - (ksize - 1) + k
            conv = conv + cw_ref[k:k + 1, cs] * gate_ref[start:start + tm, :]
        up = jnp.dot(xb, wu_ref[:, cs], preferred_element_type=F32)
        h_ref[:, cs] = (jax.nn.silu(conv) * up).astype(BF16)
    down = jnp.dot(h_ref[...], wd_ref[...], preferred_element_type=F32)
    o_ref[0] = _ln_rows(alpha * x + down, g_ref[...], b_ref[...])


def _conv_ffn(x3d, w_ffn_in, w_conv, b_conv, w_down, g, b, alpha):
    bsz, s, d = x3d.shape
    d_ff = w_down.shape[0]
    ksize = w_conv.shape[0]
    assert d_ff % FFN_FC == 0 and ksize - 1 <= FFN_HALO
    wg = w_ffn_in[:, :d_ff].astype(BF16)
    wu = w_ffn_in[:, d_ff:].astype(BF16)
    wd = w_down.astype(BF16)
    ratio = FFN_TM // FFN_HALO
    const = lambda shape: pl.BlockSpec(shape, lambda bi, i: (0, 0))
    return pl.pallas_call(
        functools.partial(_ffn_kernel, alpha=alpha, n_chunks=d_ff // FFN_FC, ksize=ksize),
        grid=(bsz, s // FFN_TM),
        in_specs=[pl.BlockSpec((1, FFN_TM, d), lambda bi, i: (bi, i, 0)),
                  pl.BlockSpec((1, FFN_HALO, d), lambda bi, i: (bi, jnp.maximum(i * ratio - 1, 0), 0)),
                  const((d, d_ff)), const((d, d_ff)), const((d_ff, d)),
                  const((ksize, d_ff)), const((1, d_ff)), const((1, d)), const((1, d))],
        out_specs=pl.BlockSpec((1, FFN_TM, d), lambda bi, i: (bi, i, 0)),
        out_shape=jax.ShapeDtypeStruct((bsz, s, d), F32),
        scratch_shapes=[pltpu.VMEM((FFN_HALO + FFN_TM, FFN_FC), F32), pltpu.VMEM((FFN_TM, d_ff), BF16)],
        compiler_params=_cparams("parallel", "arbitrary"),
        name="conv_ffn_ln",
    )(x3d, x3d, wg, wu, wd, w_conv, b_conv.reshape(1, d_ff), g.reshape(1, d), b.reshape(1, d))


def kernel(x, ln_emb_g, ln_emb_b, w_in, b_f, w_cmp1, w_cmp2, pe_cmp, w_dw, b_dw, ln_conv_g, ln_conv_b, w_out,
           ln1_g, ln1_b, w_ffn_in, w_ffn_conv, b_ffn_conv, w_ffn_down, ln2_g, ln2_b):
    bsz, s, d = x.shape
    depth = w_in.shape[0]
    alpha = (2.0 * depth) ** 0.25
    t = bsz * s

    xc = x.reshape(t, d)
    for l in range(depth):
        proj = _project(xc.reshape(bsz, s, d), ln_emb_g, ln_emb_b, *_split_w_in(w_in[l]), apply_ln=(l == 0))
        (nq, kslc, kwin, fq, fk, cmp_k, cmp_v, conv_u, vslct, vwint, fvt, gates_t, f_t) = proj[:13]
        if l == 0:
            xc = proj[13].reshape(t, d)

        kc, vct = _compress(cmp_k, cmp_v, pe_cmp[l], w_cmp1[l], w_cmp2[l])
        o_a = _nsa_attention(nq, kc, vct, kslc, vslct, kwin, vwint, gates_t)

        o_b = _conv_mixer(conv_u, w_dw[l], b_dw[l], ln_conv_g[l], ln_conv_b[l])

        c, pieces = _fox_cumsum(f_t, b_f[l])
        c_rows = c[:, :FOX_HEADS].reshape(bsz, FOX_HEADS, 1, s)
        kb = pieces[:, :, :FOX_HEADS].reshape(bsz, BIAS_PIECES, FOX_HEADS // 2, 2, s)
        kb = kb.transpose(0, 2, 4, 3, 1).reshape(bsz, FOX_HEADS // 2, s, 2 * BIAS_PIECES)
        kb = jnp.pad(kb, ((0, 0), (0, 0), (0, 0), (0, LANES - 2 * BIAS_PIECES))).astype(BF16)
        o_c = _fox_attention(fq, fk, kb, fvt, c_rows)

        xc = _out_proj(o_a.reshape(t, NSA_WIDTH), o_b.reshape(t, CONV_WIDTH), o_c.reshape(t, FOX_WIDTH),
                       xc, w_out[l], ln1_g[l], ln1_b[l], alpha)
        xc = _conv_ffn(xc.reshape(bsz, s, d), w_ffn_in[l], w_ffn_conv[l], b_ffn_conv[l], w_ffn_down[l],
                       ln2_g[l], ln2_b[l], alpha).reshape(t, d)
    return xc.reshape(bsz, s, d)
```

```python
import functools

import jax
import jax.numpy as jnp
import numpy as np
from jax import lax
from jax.experimental import pallas as pl
from jax.experimental.pallas import tpu as pltpu

HEAD_DIM = 64
NSA_HEADS = 6
NSA_KV_HEADS = 2
NSA_GROUP = 3
FOX_HEADS = 6
NSA_WIDTH = NSA_HEADS * HEAD_DIM
CONV_WIDTH = 256
FOX_WIDTH = FOX_HEADS * HEAD_DIM
CMP_BLOCK = 32
SEL_BLOCK = 64
SEL_TOPK = 16
WINDOW = 512
LN_EPS = 1e-5
NEG_INF = -1e30
SCALE = HEAD_DIM ** -0.5
LOG2E = 1.4426950408889634
BIAS_PIECES = 3
BIAS_LANE0 = HEAD_DIM
PAD_LANE = BIAS_LANE0 + 2 * BIAS_PIECES
VROWS = HEAD_DIM + 16
MASK_BIG = 2.0 ** 100

LANES = 128
VMEM_LIMIT = 56 * 1024 * 1024

BF16 = jnp.bfloat16
F32 = jnp.float32
NT_DIMS = (((1,), (1,)), ((), ()))


def _cparams(*sem):
    return pltpu.CompilerParams(dimension_semantics=sem, vmem_limit_bytes=VMEM_LIMIT)


def _ln_rows(x, g, b):
    mu = jnp.mean(x, axis=-1, keepdims=True)
    xc = x - mu
    var = jnp.mean(xc * xc, axis=-1, keepdims=True)
    return xc * lax.rsqrt(var + LN_EPS) * g + b


def _bf16_pieces(value, n=BIAS_PIECES):
    out = []
    rest = np.float32(value)
    for _ in range(n):
        bits = np.float32(rest).view(np.uint32)
        bits = (bits + np.uint32(0x7FFF) + ((bits >> np.uint32(16)) & np.uint32(1))) & np.uint32(0xFFFF0000)
        piece = bits.view(np.float32)
        out.append(float(piece))
        rest = np.float32(rest - piece)
    return out


def _position_lanes(pos):
    hi = ((pos // SEL_BLOCK) * SEL_BLOCK).astype(F32)
    lo = (pos % SEL_BLOCK).astype(F32)
    lanes = jnp.zeros((pos.shape[0], LANES), F32)
    for i in range(BIAS_PIECES):
        lanes = lanes.at[:, BIAS_LANE0 + 2 * i].set(hi)
        lanes = lanes.at[:, BIAS_LANE0 + 2 * i + 1].set(lo)
    return lanes


NSA_Q_COLS = NSA_WIDTH
NSA_K_COLS = NSA_KV_HEADS * LANES
ROWS_SPLIT = (NSA_Q_COLS, NSA_K_COLS, NSA_K_COLS, FOX_WIDTH, FOX_WIDTH)
F32_SPLIT = (128, 128, 2 * CONV_WIDTH)
T_SPLIT = (128, 128, FOX_WIDTH, 24, 8)
T_ROWS = tuple(w // HEAD_DIM * VROWS for w in T_SPLIT[:3]) + T_SPLIT[3:]


def _proj_kernel(x_ref, g_ref, b_ref, wr_ref, wf_ref, wt_ref, pos_ref,
                 nq_ref, kslc_ref, kwin_ref, fq_ref, fk_ref, cmpk_ref, cmpv_ref, conv_ref,
                 vslc_ref, vwin_ref, fv_ref, gate_ref, flog_ref, *xn_ref):
    x = x_ref[0]
    if xn_ref:
        x = _ln_rows(x, g_ref[...], b_ref[...])
        xn_ref[0][0] = x
    xb = x.astype(BF16)
    rows = jnp.dot(xb, wr_ref[...], preferred_element_type=F32)
    off = 0
    for ref, w in zip((nq_ref, kslc_ref, kwin_ref, fq_ref, fk_ref), ROWS_SPLIT):
        blk = rows[:, off:off + w]
        if ref is kslc_ref or ref is kwin_ref:
            blk = blk + pos_ref[...]
        ref[0] = blk.astype(ref.dtype)
        off += w
    f32o = jnp.dot(xb, wf_ref[...], preferred_element_type=F32)
    off = 0
    for ref, w in zip((cmpk_ref, cmpv_ref, conv_ref), F32_SPLIT):
        ref[0] = f32o[:, off:off + w]
        off += w
    tr = lax.dot_general(wt_ref[...], xb, NT_DIMS, preferred_element_type=F32)
    ones = jnp.ones((VROWS - HEAD_DIM, tr.shape[1]), F32)
    off = 0
    for ref, w in zip((vslc_ref, vwin_ref, fv_ref, gate_ref, flog_ref), T_SPLIT):
        blk = tr[off:off + w, :]
        if ref.shape[1] != w:
            blk = jnp.concatenate([part for h in range(w // HEAD_DIM)
                                   for part in (blk[h * HEAD_DIM:(h + 1) * HEAD_DIM], ones)], axis=0)
        ref[0] = blk.astype(ref.dtype)
        off += w


def _split_w_in(w_in_l):
    kvw = NSA_KV_HEADS * HEAD_DIM
    o = 0
    w_nq = w_in_l[:, o:o + NSA_WIDTH]; o += NSA_WIDTH
    w_kv = w_in_l[:, o:o + 6 * kvw]; o += 6 * kvw
    w_g = w_in_l[:, o:o + 3 * NSA_HEADS]; o += 3 * NSA_HEADS
    w_conv = w_in_l[:, o:o + 2 * CONV_WIDTH]; o += 2 * CONV_WIDTH
    w_fqkv = w_in_l[:, o:o + 3 * FOX_WIDTH]; o += 3 * FOX_WIDTH
    w_f = w_in_l[:, o:o + FOX_HEADS]
    d = w_in_l.shape[0]

    def head_blocks(w, n_heads):
        w = w.reshape(d, n_heads, HEAD_DIM)
        return jnp.pad(w, ((0, 0), (0, 0), (0, LANES - HEAD_DIM))).reshape(d, n_heads * LANES)

    w_nq = w_nq * (SCALE * LOG2E)
    kv = [w_kv[:, i * kvw:(i + 1) * kvw] for i in range(6)]
    w_fq = w_fqkv[:, :FOX_WIDTH] * (SCALE * LOG2E)
    w_fk = w_fqkv[:, FOX_WIDTH:2 * FOX_WIDTH]
    w_fv = w_fqkv[:, 2 * FOX_WIDTH:]
    w_rows = jnp.concatenate([w_nq, head_blocks(kv[2], NSA_KV_HEADS), head_blocks(kv[4], NSA_KV_HEADS),
                              w_fq, w_fk], axis=1).astype(BF16)
    w_f32 = jnp.concatenate([kv[0], kv[1], w_conv], axis=1).astype(BF16)
    w_gp = jnp.pad(w_g, ((0, 0), (0, T_SPLIT[3] - w_g.shape[1])))
    w_fp = jnp.pad(w_f, ((0, 0), (0, T_SPLIT[4] - w_f.shape[1])))
    w_t = jnp.concatenate([kv[3], kv[5], w_fv, w_gp, w_fp], axis=1).T.astype(BF16)
    return w_rows, w_f32, w_t


def _project(x3d, ln_g, ln_b, w_rows, w_f32, w_t, apply_ln, tm=512):
    b, s, d = x3d.shape
    pos_lanes = jnp.tile(_position_lanes(jnp.arange(s)), (1, NSA_KV_HEADS))
    row_spec = lambda w: pl.BlockSpec((1, tm, w), lambda bi, i: (bi, i, 0))
    t_spec = lambda w: pl.BlockSpec((1, w, tm), lambda bi, i: (bi, 0, i))
    full = lambda a: pl.BlockSpec(a.shape, lambda bi, i: (0, 0))
    out_shape = ([jax.ShapeDtypeStruct((b, s, w), BF16) for w in ROWS_SPLIT]
                 + [jax.ShapeDtypeStruct((b, s, w), F32) for w in F32_SPLIT]
                 + [jax.ShapeDtypeStruct((b, w, s), dt) for w, dt in zip(T_ROWS, (BF16, BF16, BF16, F32, F32))])
    out_specs = [row_spec(w) for w in ROWS_SPLIT + F32_SPLIT] + [t_spec(w) for w in T_ROWS]
    if apply_ln:
        out_shape.append(jax.ShapeDtypeStruct((b, s, d), F32))
        out_specs.append(row_spec(d))
    vec = pl.BlockSpec((1, d), lambda bi, i: (0, 0))
    return pl.pallas_call(
        _proj_kernel,
        grid=(b, s // tm),
        in_specs=[row_spec(d), vec, vec, full(w_rows), full(w_f32), full(w_t),
                  pl.BlockSpec((tm, NSA_K_COLS), lambda bi, i: (i, 0))],
        out_specs=out_specs,
        out_shape=out_shape,
        compiler_params=_cparams("parallel", "parallel"),
        name="in_proj",
    )(x3d, ln_g.reshape(1, d), ln_b.reshape(1, d), w_rows, w_f32, w_t, pos_lanes)


def _compress_kernel(rawk_ref, rawv_ref, pe_ref, w1_ref, w2k_ref, w2vt_ref, end_ref, kc_ref, vct_ref):
    s = rawk_ref.shape[1]
    half = s // (2 * CMP_BLOCK)
    hidden = w1_ref.shape[-1] // NSA_KV_HEADS
    outs = []
    for i, raw_ref in enumerate((rawk_ref, rawv_ref)):
        hid = jnp.zeros((2 * half, w1_ref.shape[-1]), F32)
        for p in range(CMP_BLOCK):
            rows = jnp.concatenate(
                [raw_ref[0, pl.ds(par * CMP_BLOCK + p, half, stride=2 * CMP_BLOCK), :]
                 for par in range(2)], axis=0)
            hid = hid + jnp.dot((rows + pe_ref[i, p:p + 1, :]).astype(BF16), w1_ref[i, p],
                                preferred_element_type=F32)
        outs.append(jax.nn.gelu(hid).astype(BF16))
    kc_ref[0] = jnp.concatenate(
        [jnp.dot(outs[0][:, g * hidden:(g + 1) * hidden], w2k_ref[...], preferred_element_type=F32) + end_ref[...]
         for g in range(NSA_KV_HEADS)], axis=1).astype(kc_ref.dtype)
    vct_ref[0] = jnp.concatenate(
        [lax.dot_general(w2vt_ref[...], outs[1][:, g * hidden:(g + 1) * hidden], NT_DIMS,
                         preferred_element_type=F32) for g in range(NSA_KV_HEADS)], axis=0).astype(vct_ref.dtype)


def _compress(cmp_k, cmp_v, pe, w1, w2):
    b, s, _ = cmp_k.shape
    n_cmp = s // CMP_BLOCK
    hidden = w1.shape[-1]
    pe2 = jnp.tile(pe, (1, 1, NSA_KV_HEADS))
    w1p = w1.reshape(2, CMP_BLOCK, HEAD_DIM, hidden)
    zero = jnp.zeros_like(w1p)
    w1blk = jnp.concatenate([jnp.concatenate([w1p, zero], axis=-1),
                             jnp.concatenate([zero, w1p], axis=-1)], axis=2).astype(BF16)
    w2k = jnp.pad(w2[0], ((0, 0), (0, LANES - HEAD_DIM))).astype(BF16)
    w2vt = w2[1].T.astype(BF16)
    blk = jnp.concatenate([jnp.arange(0, n_cmp, 2), jnp.arange(1, n_cmp, 2)])
    end_lanes = _position_lanes(blk * CMP_BLOCK + CMP_BLOCK - 1)
    const = lambda a: pl.BlockSpec(a.shape, lambda i: (0,) * a.ndim)
    return pl.pallas_call(
        _compress_kernel,
        grid=(b,),
        in_specs=[pl.BlockSpec((1, s, LANES), lambda i: (i, 0, 0)), pl.BlockSpec((1, s, LANES), lambda i: (i, 0, 0)),
                  const(pe2), const(w1blk), const(w2k), const(w2vt), const(end_lanes)],
        out_specs=[pl.BlockSpec((1, n_cmp, NSA_K_COLS), lambda i: (i, 0, 0)),
                   pl.BlockSpec((1, NSA_KV_HEADS * HEAD_DIM, n_cmp), lambda i: (i, 0, 0))],
        out_shape=[jax.ShapeDtypeStruct((b, n_cmp, NSA_K_COLS), BF16),
                   jax.ShapeDtypeStruct((b, NSA_KV_HEADS * HEAD_DIM, n_cmp), BF16)],
        compiler_params=_cparams("parallel"),
        name="nsa_compress",
    )(cmp_k, cmp_v, pe2, w1blk, w2k, w2vt, end_lanes)


NSA_TQ = 256
NSA_KT = 512


def _alibi_slope(h):
    return float(2.0 ** (-8.0 * (h + 1) / NSA_HEADS))


def _nsa_kernel(q_ref, kc_ref, vct_ref, kslc_ref, vslct_ref, kwin_ref, vwint_ref, gate_ref, e_ref, o_ref,
                s_buf, p_buf, idx_ref, *, seq_len):
    tq, kt = NSA_TQ, NSA_KT
    n_cmp = seq_len // CMP_BLOCK
    n_sel = seq_len // SEL_BLOCK
    n_tiles = seq_len // kt
    top_k = min(SEL_TOPK, n_sel)
    kw = WINDOW + tq
    wide = NSA_GROUP * tq
    qi = pl.program_id(1)
    q0 = qi * tq
    t_row = q0 + lax.broadcasted_iota(jnp.int32, (1, tq), 1)
    t_f = t_row.astype(F32)
    lane = lax.broadcasted_iota(jnp.int32, (1, LANES), 1)
    cur = t_row // SEL_BLOCK
    groups = range(NSA_KV_HEADS)

    def tile3(a):
        return jnp.concatenate([a] * NSA_GROUP, axis=1)

    q1, cq = [], []
    for g in groups:
        rows, consts = [], []
        for j in range(NSA_GROUP):
            h = g * NSA_GROUP + j
            slope2 = _alibi_slope(h) * LOG2E
            sl = jnp.where(lane == PAD_LANE, -MASK_BIG, jnp.zeros((1, LANES), F32))
            for i, piece in enumerate(_bf16_pieces(slope2)):
                sl = jnp.where((lane == BIAS_LANE0 + 2 * i) | (lane == BIAS_LANE0 + 2 * i + 1), piece, sl)
            qv = q_ref[0, :, (h // 2) * LANES:(h // 2 + 1) * LANES].astype(F32)
            if h % 2:
                qv = pltpu.roll(qv, HEAD_DIM, axis=1)
            rows.append((jnp.where(lane < HEAD_DIM, qv, 0.0) + sl).astype(BF16))
            consts.append(-slope2 * t_f)
        q1.append(jnp.concatenate(rows, axis=0))
        cq.append(jnp.concatenate(consts, axis=1))

    def softmax_weights(s, valid):
        ps = []
        for j in range(NSA_GROUP):
            sj = jnp.where(valid, s[:, j * tq:(j + 1) * tq], NEG_INF)
            m = jnp.max(sj, axis=0, keepdims=True)
            p = jnp.exp2(sj - m)
            l = jnp.sum(p, axis=0, keepdims=True)
            ps.append(p * jnp.where(m > 0.5 * NEG_INF, 1.0 / l, 0.0))
        return jnp.concatenate(ps, axis=1)

    def softmax_out(s, vt):
        m = jnp.max(s, axis=0, keepdims=True)
        pv = jnp.dot(vt, jnp.exp2(s - m).astype(BF16), preferred_element_type=F32)
        return pv[:HEAD_DIM] * (1.0 / pv[HEAD_DIM:HEAD_DIM + 1])

    def vrows(g):
        return slice(g * VROWS, (g + 1) * VROWS)

    def kcols(g):
        return slice(g * LANES, (g + 1) * LANES)

    r = lax.broadcasted_iota(jnp.int32, (n_cmp, tq), 0)
    blk_id = jnp.where(r < n_sel, 2 * r, 2 * (r - n_sel) + 1)
    valid_c = t_row - (blk_id * CMP_BLOCK + CMP_BLOCK - 1) >= 0
    o_c, score = [], []
    jid = lax.broadcasted_iota(jnp.int32, (n_sel, tq), 0)
    forced = (jid == 0) | (jid == cur)
    for g in groups:
        s_c = lax.dot_general(kc_ref[0, :, kcols(g)], q1[g], NT_DIMS, preferred_element_type=F32)
        p_c = softmax_weights(s_c, valid_c)
        o_c.append(jnp.dot(vct_ref[0, g * HEAD_DIM:(g + 1) * HEAD_DIM, :], p_c.astype(BF16),
                           preferred_element_type=F32))
        imp = p_c[:, 0:tq] + p_c[:, tq:2 * tq] + p_c[:, 2 * tq:3 * tq]
        imp = imp[0:n_sel, :] + imp[n_sel:2 * n_sel, :]
        score.append(jnp.where(jid > cur, -1.0, jnp.where(forced, NSA_GROUP + 1.0, imp)))

    for _ in range(top_k):
        for g in groups:
            mx = jnp.max(score[g], axis=0, keepdims=True)
            first = jnp.min(jnp.where(score[g] == mx, jid, n_sel), axis=0, keepdims=True)
            score[g] = jnp.where(jid == first, -2.0, score[g])
    sel = [jnp.where(score[g] == -2.0, 1.0, 0.0) for g in groups]

    w0 = pl.multiple_of(q0, tq)
    row = lax.broadcasted_iota(jnp.int32, (tq, tq), 0)
    col = lax.broadcasted_iota(jnp.int32, (tq, tq), 1)
    near = tile3(col < row)
    causal = tile3(row <= col)
    o_w = []
    for g in groups:
        s_w = lax.dot_general(kwin_ref[0, pl.ds(w0, kw), kcols(g)], q1[g], NT_DIMS, preferred_element_type=F32)
        s_w = jnp.concatenate([jnp.where(near, s_w[:tq], -jnp.inf), s_w[tq:WINDOW],
                               jnp.where(causal, s_w[WINDOW:], -jnp.inf)], axis=0)
        o_w.append(softmax_out(s_w, vwint_ref[0, vrows(g), pl.ds(w0, kw)]))

    kpos = q0 + lax.broadcasted_iota(jnp.int32, (tq, tq), 0)
    valid_o = tile3((kpos // SEL_BLOCK == cur) & (kpos <= t_row))
    states, q2, counts = [], [], []
    past = jid < cur
    tile_of_blk = (lax.broadcasted_iota(jnp.int32, (max(n_tiles, 8), n_sel), 1) // (kt // SEL_BLOCK)
                   == lax.broadcasted_iota(jnp.int32, (max(n_tiles, 8), n_sel), 0)).astype(BF16)
    for g in groups:
        s = lax.dot_general(kslc_ref[0, pl.ds(q0, tq), kcols(g)], q1[g], NT_DIMS, preferred_element_type=F32)
        s = jnp.where(valid_o, s, -jnp.inf)
        m0 = jnp.max(s, axis=0, keepdims=True) + cq[g]
        p = jnp.exp2(s - (m0 - cq[g]))
        acc0 = jnp.dot(vslct_ref[0, vrows(g), pl.ds(q0, tq)], p.astype(BF16), preferred_element_type=F32)
        p_buf[g] = jnp.zeros((kt, wide), BF16)
        states.append((jnp.ones((1, wide), F32), m0, acc0))

        sel_sw = jnp.where(past, sel[g], 0.0)
        sel_pad = jnp.concatenate([sel_sw, jnp.zeros((LANES - n_sel, tq), F32)], axis=0) if n_sel < LANES else sel_sw
        mask_lanes = ((sel_pad.T - 1.0) * MASK_BIG).astype(BF16)
        q2.append(jnp.concatenate([q1[g], jnp.concatenate([mask_lanes] * NSA_GROUP, axis=0)], axis=1))
        per_tile = jnp.dot(tile_of_blk, sel_sw.astype(BF16), preferred_element_type=F32)
        counts.append(jnp.dot(per_tile.astype(BF16), jnp.ones((tq, LANES), BF16), preferred_element_type=F32))

    n_act = []
    for g in groups:
        for t in range(n_tiles + 1):
            idx_ref[g, t] = n_tiles
        c = jnp.int32(0)
        for t in range(n_tiles):
            idx_ref[g, c] = t
            c = c + (counts[g][t, 0] > 0.5).astype(jnp.int32)
        idx_ref[g, c] = n_tiles
        n_act.append(c)
    n_trips = jnp.maximum(n_act[0], n_act[1])

    def key_start(tile):
        return pl.multiple_of(jnp.minimum(tile, n_tiles - 1) * kt, kt)

    def stage_a(g, tile):
        e0 = pl.multiple_of(tile * kt, kt)
        kk = jnp.concatenate([kslc_ref[0, pl.ds(key_start(tile), kt), kcols(g)], e_ref[pl.ds(e0, kt), :]],
                             axis=1)
        s = lax.dot_general(kk, q2[g], NT_DIMS, preferred_element_type=F32)
        s_buf[g] = s
        return jnp.max(s, axis=0, keepdims=True)

    def stage_b(g, tmax, m_i):
        m_new = jnp.maximum(m_i, tmax + cq[g])
        p_buf[g] = jnp.exp2(s_buf[g] - (m_new - cq[g])).astype(BF16)
        return jnp.exp2(m_i - m_new), m_new

    def stage_c(g, tile, alpha, acc):
        pv = jnp.dot(vslct_ref[0, vrows(g), pl.ds(key_start(tile), kt)], p_buf[g], preferred_element_type=F32)
        return alpha * acc + pv

    def step(g, i, state, run_a):
        tmax, alpha, m_i, acc = state
        acc = stage_c(g, idx_ref[g, jnp.maximum(i - 2, 0)], alpha, acc)
        alpha, m_i = stage_b(g, tmax, m_i)
        if run_a:
            tmax = stage_a(g, idx_ref[g, i])
        return tmax, alpha, m_i, acc

    n_trips = jnp.maximum(n_trips, 1)
    states = tuple((stage_a(g, idx_ref[g, 0]),) + states[g] for g in groups)
    states = lax.fori_loop(1, n_trips, lambda i, st: tuple(step(g, i, st[g], True) for g in groups), states)
    states = tuple(step(g, n_trips, states[g], False) for g in groups)

    out_rows = [None] * NSA_HEADS
    for g in groups:
        _, alpha, _, acc = states[g]
        acc = stage_c(g, idx_ref[g, n_trips - 1], alpha, acc)
        o_s = acc[:HEAD_DIM] * (1.0 / acc[HEAD_DIM:HEAD_DIM + 1])
        for j in range(NSA_GROUP):
            h = g * NSA_GROUP + j
            gt = jax.nn.sigmoid(gate_ref[0, 3 * h:3 * h + 3, :])
            sl = slice(j * tq, (j + 1) * tq)
            out_rows[h] = gt[0:1] * o_c[g][:, sl] + gt[1:2] * o_s[:, sl] + gt[2:3] * o_w[g][:, sl]
    o_ref[0] = jnp.concatenate(out_rows, axis=0).T.astype(o_ref.dtype)


def _nsa_attention(nq, kc, vct, kslc, vslct, kwin, vwint, gates_t):
    b, s, _ = nq.shape
    n_cmp = s // CMP_BLOCK
    n_sel = s // SEL_BLOCK
    n_tiles = s // NSA_KT
    assert SEL_TOPK < n_sel <= LANES and s % NSA_KT == 0 and NSA_TQ <= WINDOW
    e_sel = jnp.arange(s)[:, None] // SEL_BLOCK == jnp.arange(LANES)[None, :]
    e_sel = jnp.concatenate([e_sel, jnp.ones((NSA_KT, LANES), bool)], axis=0).astype(BF16)
    pad_rows = jnp.zeros((WINDOW, LANES), BF16).at[:, PAD_LANE].set(1)
    pad_rows = jnp.broadcast_to(jnp.tile(pad_rows, (1, NSA_KV_HEADS))[None], (b, WINDOW, NSA_K_COLS))
    kwin = jnp.concatenate([pad_rows, kwin], axis=1)
    vwint = jnp.pad(vwint, ((0, 0), (0, 0), (WINDOW, 0)))
    vr = NSA_KV_HEADS * VROWS
    per_b = lambda shape: pl.BlockSpec((1,) + shape, lambda bi, i: (bi, 0, 0))
    wide = NSA_GROUP * NSA_TQ
    return pl.pallas_call(
        functools.partial(_nsa_kernel, seq_len=s),
        grid=(b, s // NSA_TQ),
        in_specs=[pl.BlockSpec((1, NSA_TQ, NSA_Q_COLS), lambda bi, i: (bi, i, 0)),
                  per_b((n_cmp, NSA_K_COLS)), per_b((LANES, n_cmp)),
                  per_b((s, NSA_K_COLS)), per_b((vr, s)),
                  per_b((s + WINDOW, NSA_K_COLS)), per_b((vr, s + WINDOW)),
                  pl.BlockSpec((1, gates_t.shape[1], NSA_TQ), lambda bi, i: (bi, 0, i)),
                  pl.BlockSpec(e_sel.shape, lambda bi, i: (0, 0))],
        out_specs=pl.BlockSpec((1, NSA_TQ, NSA_WIDTH), lambda bi, i: (bi, i, 0)),
        out_shape=jax.ShapeDtypeStruct((b, s, NSA_WIDTH), BF16),
        scratch_shapes=[pltpu.VMEM((NSA_KV_HEADS, NSA_KT, wide), F32), pltpu.VMEM((NSA_KV_HEADS, NSA_KT, wide), BF16),
                        pltpu.SMEM((NSA_KV_HEADS, n_tiles + 1), jnp.int32)],
        compiler_params=_cparams("parallel", "arbitrary"),
        name="nsa_attention",
    )(nq, kc, vct, kslc, vslct, kwin, vwint, gates_t, e_sel)


def _cumsum_kernel(f_ref, b_ref, c_ref, piece_ref):
    rows = f_ref.shape[1]
    per_head = rows // 8
    hi = lax.Precision.HIGHEST
    ri = lax.broadcasted_iota(jnp.int32, (LANES, LANES), 0)
    ci = lax.broadcasted_iota(jnp.int32, (LANES, LANES), 1)
    upper = (ri <= ci).astype(F32)
    rr = lax.broadcasted_iota(jnp.int32, (rows, rows), 0)
    rc = lax.broadcasted_iota(jnp.int32, (rows, rows), 1)
    earlier = ((rc < rr) & (rc // per_head == rr // per_head)).astype(F32)
    lf = jax.nn.log_sigmoid(f_ref[0] + b_ref[...])
    within = jnp.dot(lf, upper, preferred_element_type=F32, precision=hi)
    total = jnp.dot(lf, jnp.ones((LANES, LANES), F32), preferred_element_type=F32, precision=hi)
    carry = jnp.dot(earlier, total, preferred_element_type=F32, precision=hi)
    c = (within + carry) * LOG2E
    c_ref[0] = c
    rest = -c
    for i in range(BIAS_PIECES):
        piece = rest.astype(BF16).astype(F32)
        piece_ref[0, i] = piece
        rest = rest - piece


def _fox_cumsum(f_t, b_f):
    b, r, s = f_t.shape
    assert r == 8
    rows = r * (s // LANES)
    bias = jnp.repeat(jnp.pad(b_f, (0, r - b_f.shape[0])), s // LANES)
    bias = jnp.broadcast_to(bias[:, None], (rows, LANES)).astype(F32)
    c, pieces = pl.pallas_call(
        _cumsum_kernel,
        grid=(b,),
        in_specs=[pl.BlockSpec((1, rows, LANES), lambda i: (i, 0, 0)),
                  pl.BlockSpec((rows, LANES), lambda i: (0, 0))],
        out_specs=[pl.BlockSpec((1, rows, LANES), lambda i: (i, 0, 0)),
                   pl.BlockSpec((1, BIAS_PIECES, rows, LANES), lambda i: (i, 0, 0, 0))],
        out_shape=[jax.ShapeDtypeStruct((b, rows, LANES), F32),
                   jax.ShapeDtypeStruct((b, BIAS_PIECES, rows, LANES), F32)],
        compiler_params=_cparams("parallel"),
        name="fox_cumsum",
    )(f_t.reshape(b, rows, LANES), bias)
    return c.reshape(b, r, s), pieces.reshape(b, BIAS_PIECES, r, s)


FOX_TQ = 1024
FOX_KT = 1024


def _fox_kernel(q_ref, k_ref, kb_ref, vt_ref, cq_ref, o_ref, s_buf, p_buf):
    tq, kt, vr = FOX_TQ, FOX_KT, VROWS
    q0 = pl.program_id(2) * tq
    n_full = q0 // kt
    lane = lax.broadcasted_iota(jnp.int32, (tq, LANES), 1)
    t_row = q0 + lax.broadcasted_iota(jnp.int32, (1, tq), 1)
    qh = []
    for hh in range(2):
        pick = (lane >= hh * BIAS_PIECES) & (lane < (hh + 1) * BIAS_PIECES)
        qh.append(jnp.concatenate([jnp.where(lane // HEAD_DIM == hh, q_ref[0], 0).astype(BF16),
                                   jnp.where(pick, 1.0, 0.0).astype(BF16)], axis=1))
    cq = [cq_ref[0, hh] for hh in range(2)]

    def stage_a(hh, j, masked):
        k0 = pl.multiple_of(j * kt, kt)
        kk = jnp.concatenate([k_ref[0, pl.ds(k0, kt), :], kb_ref[0, 0, pl.ds(k0, kt), :]], axis=1)
        s = lax.dot_general(kk, qh[hh], NT_DIMS, preferred_element_type=F32)
        if masked:
            s = jnp.where(t_row >= k0 + lax.broadcasted_iota(jnp.int32, (kt, tq), 0), s, -jnp.inf)
        s_buf[hh] = s
        return jnp.max(s, axis=0, keepdims=True)

    def stage_b(hh, tmax, m_i):
        m_new = jnp.maximum(m_i, tmax + cq[hh])
        p_buf[hh] = jnp.exp2(s_buf[hh] - (m_new - cq[hh])).astype(BF16)
        return jnp.exp2(m_i - m_new), m_new

    def stage_c(hh, j, alpha, acc):
        k0 = pl.multiple_of(j * kt, kt)
        pv = jnp.dot(vt_ref[0, hh * vr:(hh + 1) * vr, pl.ds(k0, kt)], p_buf[hh], preferred_element_type=F32)
        return alpha * acc + pv

    def step(hh, i, state, run_a):
        tmax, m_i, acc = state
        alpha, m_i = stage_b(hh, tmax, m_i)
        acc = stage_c(hh, jnp.where(i == 0, n_full, jnp.maximum(i - 1, 0)), alpha, acc)
        if run_a:
            tmax = stage_a(hh, i, False)
        return tmax, m_i, acc

    states = []
    for hh in range(2):
        tmax = stage_a(hh, n_full, True)
        states.append((tmax, jnp.full((1, tq), NEG_INF, F32), jnp.zeros((vr, tq), F32)))
    states = lax.fori_loop(0, n_full, lambda i, st: tuple(step(hh, i, st[hh], True) for hh in range(2)),
                           tuple(states))
    states = tuple(step(hh, n_full, states[hh], False) for hh in range(2))
    outs = []
    for hh in range(2):
        acc = states[hh][2]
        outs.append(acc[:HEAD_DIM] / acc[HEAD_DIM:HEAD_DIM + 1])
    o_ref[0] = jnp.concatenate(outs, axis=0).T.astype(o_ref.dtype)


def _fox_attention(fq, fk, kb, fvt, c_rows):
    b, s, _ = fq.shape
    n_pair = FOX_HEADS // 2
    assert FOX_KT % FOX_TQ == 0 and s % FOX_KT == 0
    return pl.pallas_call(
        _fox_kernel,
        grid=(b, n_pair, s // FOX_TQ),
        in_specs=[pl.BlockSpec((1, FOX_TQ, LANES), lambda bi, p, i: (bi, i, p)),
                  pl.BlockSpec((1, s, LANES), lambda bi, p, i: (bi, 0, p)),
                  pl.BlockSpec((1, 1, s, LANES), lambda bi, p, i: (bi, p, 0, 0)),
                  pl.BlockSpec((1, 2 * VROWS, s), lambda bi, p, i: (bi, p, 0)),
                  pl.BlockSpec((1, 2, 1, FOX_TQ), lambda bi, p, i: (bi, p, 0, i))],
        out_specs=pl.BlockSpec((1, FOX_TQ, LANES), lambda bi, p, i: (bi, i, p)),
        out_shape=jax.ShapeDtypeStruct((b, s, FOX_WIDTH), BF16),
        scratch_shapes=[pltpu.VMEM((2, FOX_KT, FOX_TQ), F32), pltpu.VMEM((2, FOX_KT, FOX_TQ), BF16)],
        compiler_params=_cparams("parallel", "parallel", "arbitrary"),
        name="fox_attention",
    )(fq, fk, kb, fvt, c_rows)


CONV_TS = 512
CONV_HALO = 32


def _conv_kernel(u_ref, uh_ref, w_ref, b_ref, g_ref, be_ref, o_ref, y_ref, sh_ref, *, ksize):
    ts, halo, cw = CONV_TS, CONV_HALO, CONV_WIDTH

    def glu(u):
        return u[:, :cw] * jax.nn.sigmoid(u[:, cw:])

    first = pl.program_id(1) == 0
    y_ref[0:halo, :] = jnp.where(first, 0.0, glu(uh_ref[0]))
    y_ref[halo:halo + ts, :] = glu(u_ref[0])
    span = ts + halo - 8
    for off in range(1, 8):
        sh_ref[off - 1, 0:span, :] = y_ref[off:off + span, :]
    acc = jnp.zeros((ts, cw), F32)
    for k in range(ksize):
        start = halo - (ksize - 1) + k
        a, off = divmod(start, 8)
        tap = y_ref[start:start + ts, :] if off == 0 else sh_ref[off - 1, 8 * a:8 * a + ts, :]
        acc = acc + w_ref[k:k + 1, :] * tap
    hn = _ln_rows(acc + b_ref[...], g_ref[...], be_ref[...])
    o_ref[0] = jax.nn.silu(hn).astype(o_ref.dtype)


def _conv_mixer(u, w_dw, b_dw, ln_g, ln_b):
    b, s, _ = u.shape
    ksize = w_dw.shape[0]
    assert ksize - 1 <= CONV_HALO
    ratio = CONV_TS // CONV_HALO
    vec = lambda a: a.reshape(1, CONV_WIDTH)
    const = lambda shape: pl.BlockSpec(shape, lambda bi, i: (0, 0))
    return pl.pallas_call(
        functools.partial(_conv_kernel, ksize=ksize),
        grid=(b, s // CONV_TS),
        in_specs=[pl.BlockSpec((1, CONV_TS, 2 * CONV_WIDTH), lambda bi, i: (bi, i, 0)),
                  pl.BlockSpec((1, CONV_HALO, 2 * CONV_WIDTH), lambda bi, i: (bi, jnp.maximum(i * ratio - 1, 0), 0)),
                  const((ksize, CONV_WIDTH)), const((1, CONV_WIDTH)), const((1, CONV_WIDTH)),
                  const((1, CONV_WIDTH))],
        out_specs=pl.BlockSpec((1, CONV_TS, CONV_WIDTH), lambda bi, i: (bi, i, 0)),
        out_shape=jax.ShapeDtypeStruct((b, s, CONV_WIDTH), BF16),
        scratch_shapes=[pltpu.VMEM((CONV_HALO + CONV_TS, CONV_WIDTH), F32),
                        pltpu.VMEM((7, CONV_HALO + CONV_TS - 8, CONV_WIDTH), F32)],
        compiler_params=_cparams("parallel", "arbitrary"),
        name="conv_mixer",
    )(u, u, w_dw, vec(b_dw), vec(ln_g), vec(ln_b))


def _outproj_kernel(oa_ref, ob_ref, oc_ref, x_ref, wa_ref, wb_ref, wc_ref, g_ref, b_ref, o_ref, *, alpha):
    mix = jnp.dot(oa_ref[...], wa_ref[...], preferred_element_type=F32)
    mix = mix + jnp.dot(ob_ref[...], wb_ref[...], preferred_element_type=F32)
    mix = mix + jnp.dot(oc_ref[...], wc_ref[...], preferred_element_type=F32)
    o_ref[...] = _ln_rows(alpha * x_ref[...] + mix, g_ref[...], b_ref[...])


def _out_proj(o_a, o_b, o_c, x2d, w_out, g, b, alpha, tm=512):
    t, d = x2d.shape
    wa = w_out[:NSA_WIDTH].astype(BF16)
    wb = w_out[NSA_WIDTH:NSA_WIDTH + CONV_WIDTH].astype(BF16)
    wc = w_out[NSA_WIDTH + CONV_WIDTH:].astype(BF16)
    rows = lambda w: pl.BlockSpec((tm, w), lambda i: (i, 0))
    full = lambda a: pl.BlockSpec(a.shape, lambda i: (0, 0))
    return pl.pallas_call(
        functools.partial(_outproj_kernel, alpha=alpha),
        grid=(t // tm,),
        in_specs=[rows(NSA_WIDTH), rows(CONV_WIDTH), rows(FOX_WIDTH), rows(d), full(wa), full(wb), full(wc),
                  pl.BlockSpec((1, d), lambda i: (0, 0)), pl.BlockSpec((1, d), lambda i: (0, 0))],
        out_specs=rows(d),
        out_shape=jax.ShapeDtypeStruct((t, d), F32),
        compiler_params=_cparams("parallel"),
        name="out_proj_ln",
    )(o_a, o_b, o_c, x2d, wa, wb, wc, g.reshape(1, d), b.reshape(1, d))


FFN_TM = 1024
FFN_HALO = 16
FFN_FC = 256


def _ffn_kernel(x_ref, xh_ref, wg_ref, wu_ref, wd_ref, cw_ref, cb_ref, g_ref, b_ref, o_ref, gate_ref, h_ref,
                *, alpha, n_chunks, ksize):
    tm, halo, fc = FFN_TM, FFN_HALO, FFN_FC
    x = x_ref[0]
    xb = x.astype(BF16)
    xeb = jnp.concatenate([xh_ref[0].astype(BF16), xb], axis=0)
    first = pl.program_id(1) == 0
    for c in range(n_chunks):
        cs = slice(c * fc, (c + 1) * fc)
        gate = jnp.dot(xeb, wg_ref[:, cs], preferred_element_type=F32)
        gate_ref[0:halo, :] = jnp.where(first, 0.0, gate[0:halo])
        gate_ref[halo:halo + tm, :] = gate[halo:]
        conv = cb_ref[:, cs]
        for k in range(ksize):
            start = halo - (ksize - 1) + k
            conv = conv + cw_ref[k:k + 1, cs] * gate_ref[start:start + tm, :]
        up = jnp.dot(xb, wu_ref[:, cs], preferred_element_type=F32)
        h_ref[:, cs] = (jax.nn.silu(conv) * up).astype(BF16)
    down = jnp.dot(h_ref[...], wd_ref[...], preferred_element_type=F32)
    o_ref[0] = _ln_rows(alpha * x + down, g_ref[...], b_ref[...])


def _conv_ffn(x3d, w_ffn_in, w_conv, b_conv, w_down, g, b, alpha):
    bsz, s, d = x3d.shape
    d_ff = w_down.shape[0]
    ksize = w_conv.shape[0]
    assert d_ff % FFN_FC == 0 and ksize - 1 <= FFN_HALO
    wg = w_ffn_in[:, :d_ff].astype(BF16)
    wu = w_ffn_in[:, d_ff:].astype(BF16)
    wd = w_down.astype(BF16)
    ratio = FFN_TM // FFN_HALO
    const = lambda shape: pl.BlockSpec(shape, lambda bi, i: (0, 0))
    return pl.pallas_call(
        functools.partial(_ffn_kernel, alpha=alpha, n_chunks=d_ff // FFN_FC, ksize=ksize),
        grid=(bsz, s // FFN_TM),
        in_specs=[pl.BlockSpec((1, FFN_TM, d), lambda bi, i: (bi, i, 0)),
                  pl.BlockSpec((1, FFN_HALO, d), lambda bi, i: (bi, jnp.maximum(i * ratio - 1, 0), 0)),
                  const((d, d_ff)), const((d, d_ff)), const((d_ff, d)),
                  const((ksize, d_ff)), const((1, d_ff)), const((1, d)), const((1, d))],
        out_specs=pl.BlockSpec((1, FFN_TM, d), lambda bi, i: (bi, i, 0)),
        out_shape=jax.ShapeDtypeStruct((bsz, s, d), F32),
        scratch_shapes=[pltpu.VMEM((FFN_HALO + FFN_TM, FFN_FC), F32), pltpu.VMEM((FFN_TM, d_ff), BF16)],
        compiler_params=_cparams("parallel", "arbitrary"),
        name="conv_ffn_ln",
    )(x3d, x3d, wg, wu, wd, w_conv, b_conv.reshape(1, d_ff), g.reshape(1, d), b.reshape(1, d))


def kernel(x, ln_emb_g, ln_emb_b, w_in, b_f, w_cmp1, w_cmp2, pe_cmp, w_dw, b_dw, ln_conv_g, ln_conv_b, w_out,
           ln1_g, ln1_b, w_ffn_in, w_ffn_conv, b_ffn_conv, w_ffn_down, ln2_g, ln2_b):
    bsz, s, d = x.shape
    depth = w_in.shape[0]
    alpha = (2.0 * depth) ** 0.25
    t = bsz * s

    xc = x.reshape(t, d)
    for l in range(depth):
        proj = _project(xc.reshape(bsz, s, d), ln_emb_g, ln_emb_b, *_split_w_in(w_in[l]), apply_ln=(l == 0))
        (nq, kslc, kwin, fq, fk, cmp_k, cmp_v, conv_u, vslct, vwint, fvt, gates_t, f_t) = proj[:13]
        if l == 0:
            xc = proj[13].reshape(t, d)

        kc, vct = _compress(cmp_k, cmp_v, pe_cmp[l], w_cmp1[l], w_cmp2[l])
        o_a = _nsa_attention(nq, kc, vct, kslc, vslct, kwin, vwint, gates_t)

        o_b = _conv_mixer(conv_u, w_dw[l], b_dw[l], ln_conv_g[l], ln_conv_b[l])

        c, pieces = _fox_cumsum(f_t, b_f[l])
        c_rows = c[:, :FOX_HEADS].reshape(bsz, FOX_HEADS, 1, s)
        kb = pieces[:, :, :FOX_HEADS].reshape(bsz, BIAS_PIECES, FOX_HEADS // 2, 2, s)
        kb = kb.transpose(0, 2, 4, 3, 1).reshape(bsz, FOX_HEADS // 2, s, 2 * BIAS_PIECES)
        kb = jnp.pad(kb, ((0, 0), (0, 0), (0, 0), (0, LANES - 2 * BIAS_PIECES))).astype(BF16)
        o_c = _fox_attention(fq, fk, kb, fvt, c_rows)

        xc = _out_proj(o_a.reshape(t, NSA_WIDTH), o_b.reshape(t, CONV_WIDTH), o_c.reshape(t, FOX_WIDTH),
                       xc, w_out[l], ln1_g[l], ln1_b[l], alpha)
        xc = _conv_ffn(xc.reshape(bsz, s, d), w_ffn_in[l], w_ffn_conv[l], b_ffn_conv[l], w_ffn_down[l],
                       ln2_g[l], ln2_b[l], alpha).reshape(t, d)
    return xc.reshape(bsz, s, d)
```
